```python
import jax, jax.numpy as jnp
from jax import lax
import numpy as np

D_MODEL = 1024
BATCH = 1
SEQ = 16384
DEPTH = 2

MLA_HEADS = 16
MLA_NOPE_DIM = 64
MLA_ROPE_DIM = 32
MLA_QK_DIM = MLA_NOPE_DIM + MLA_ROPE_DIM
MLA_V_DIM = 64
MLA_Q_RANK = 768
MLA_KV_RANK = 256
ROPE_THETA = 10000.0
Q_BLOCK = 128
LRU_WIDTH = 1024
LRU_BLOCKS = 4
LRU_BLOCK_W = LRU_WIDTH // LRU_BLOCKS
CONV_WIDTH = 4
LRU_C = 8.0
SWA_HEADS = 16
SWA_KV_HEADS = 2
SWA_GROUP = SWA_HEADS // SWA_KV_HEADS
SWA_HEAD_DIM = 64
WINDOW = 128
N_BRANCH = 3
D_FF = 4 * D_MODEL
EPS = 1e-6

IN_SIZES = (MLA_Q_RANK, MLA_KV_RANK, MLA_ROPE_DIM, LRU_WIDTH, LRU_WIDTH,
            SWA_HEADS * SWA_HEAD_DIM, SWA_KV_HEADS * SWA_HEAD_DIM, SWA_KV_HEADS * SWA_HEAD_DIM,
            N_BRANCH * D_MODEL)
D_IN = sum(IN_SIZES)

kernel_name = "hybrid_mla_rglru_swa_gated_block"


def _split_points():
    pts, acc = [], 0
    for size in IN_SIZES[:-1]:
        acc += size
        pts.append(acc)
    return pts


def rms_norm(x, gain):
    xf = x.astype(jnp.float32)
    y = xf * lax.rsqrt(jnp.mean(jnp.square(xf), axis=-1, keepdims=True) + EPS)
    return (y * gain.astype(jnp.float32)).astype(x.dtype)


def rope_tables(seq):
    pos = jnp.arange(seq, dtype=jnp.float32)
    inv = ROPE_THETA ** (-jnp.arange(0, MLA_ROPE_DIM, 2, dtype=jnp.float32) / MLA_ROPE_DIM)
    ang = pos[:, None] * inv[None, :]
    return jnp.cos(ang), jnp.sin(ang)


def apply_rope(x, cos, sin):
    xf = x.astype(jnp.float32)
    x1, x2 = jnp.split(xf, 2, axis=-1)
    c = cos[None, :, None, :]
    s = sin[None, :, None, :]
    return jnp.concatenate([x1 * c - x2 * s, x2 * c + x1 * s], axis=-1).astype(x.dtype)


def causal_block_attention(q, k, v, scale):
    B, S, H, dq = q.shape
    dv = v.shape[-1]
    nb = S // Q_BLOCK
    qb = q.reshape(B, nb, Q_BLOCK, H, dq).swapaxes(0, 1)
    starts = jnp.arange(nb) * Q_BLOCK
    k_pos = jnp.arange(S)

    def one_block(args):
        q_blk, start = args
        s = jnp.einsum('bqhd,bkhd->bhqk', q_blk, k).astype(jnp.float32) * scale
        mask = k_pos[None, :] <= (start + jnp.arange(Q_BLOCK))[:, None]
        p = jax.nn.softmax(jnp.where(mask, s, -jnp.inf), axis=-1)
        return jnp.einsum('bhqk,bkhd->bqhd', p.astype(v.dtype), v)

    out = lax.map(one_block, (qb, starts))
    return out.swapaxes(0, 1).reshape(B, S, H * dv)


def mla_branch(c_q, c_kv, k_rope, q_a_norm, kv_a_norm, w_uq, w_ukv, q_norm, k_norm, cos, sin):
    B, S, _ = c_q.shape
    q = (rms_norm(c_q, q_a_norm) @ w_uq).reshape(B, S, MLA_HEADS, MLA_QK_DIM)
    kv = (rms_norm(c_kv, kv_a_norm) @ w_ukv).reshape(B, S, MLA_HEADS, MLA_NOPE_DIM + MLA_V_DIM)
    k_nope, v = kv[..., :MLA_NOPE_DIM], kv[..., MLA_NOPE_DIM:]
    q_nope = rms_norm(q[..., :MLA_NOPE_DIM], q_norm[:MLA_NOPE_DIM])
    q_pe = apply_rope(rms_norm(q[..., MLA_NOPE_DIM:], q_norm[MLA_NOPE_DIM:]), cos, sin)
    k_nope = rms_norm(k_nope, k_norm[:MLA_NOPE_DIM])
    k_pe = apply_rope(rms_norm(k_rope, k_norm[MLA_NOPE_DIM:])[:, :, None, :], cos, sin)
    q = jnp.concatenate([q_nope, q_pe], axis=-1)
    k = jnp.concatenate([k_nope, jnp.broadcast_to(k_pe, (B, S, MLA_HEADS, MLA_ROPE_DIM))], axis=-1)
    return causal_block_attention(q, k, v, MLA_QK_DIM ** -0.5)


def block_diag_linear(x, w, b):
    B, S, _ = x.shape
    xb = x.reshape(B, S, LRU_BLOCKS, LRU_BLOCK_W)
    return jnp.einsum('bsnd,nde->bsne', xb, w).reshape(B, S, LRU_WIDTH) + b


def rglru_branch(xr, gate, conv_w, conv_b, w_rg_a, b_rg_a, w_rg_x, b_rg_x, lru_lambda):
    xc = lax.conv_general_dilated(
        xr, conv_w[:, None, :].astype(xr.dtype), window_strides=(1,),
        padding=[(CONV_WIDTH - 1, 0)], dimension_numbers=('NWC', 'WIO', 'NWC'),
        feature_group_count=LRU_WIDTH) + conv_b
    r = jax.nn.sigmoid(block_diag_linear(xc, w_rg_a, b_rg_a).astype(jnp.float32))
    i = jax.nn.sigmoid(block_diag_linear(xc, w_rg_x, b_rg_x).astype(jnp.float32))
    log_a = -LRU_C * r * jax.nn.softplus(-lru_lambda.astype(jnp.float32))
    a = jnp.exp(log_a)
    b = jnp.sqrt(-jnp.expm1(2.0 * log_a)) * (i * xc.astype(jnp.float32))

    def combine(c1, c2):
        a1, b1 = c1
        a2, b2 = c2
        return a1 * a2, a2 * b1 + b2

    _, h = lax.associative_scan(combine, (a, b), axis=1)
    return (h * jax.nn.gelu(gate.astype(jnp.float32))).astype(xr.dtype)


def swa_branch(q, k, v, q_norm, k_norm, sinks):
    B, S, _ = q.shape
    nb = S // WINDOW
    q = rms_norm(q.reshape(B, S, SWA_HEADS, SWA_HEAD_DIM), q_norm)
    k = rms_norm(k.reshape(B, S, SWA_KV_HEADS, SWA_HEAD_DIM), k_norm)
    v = v.reshape(B, S, SWA_KV_HEADS, SWA_HEAD_DIM)
    qb = q.reshape(B, nb, WINDOW, SWA_KV_HEADS, SWA_GROUP, SWA_HEAD_DIM)
    kb = k.reshape(B, nb, WINDOW, SWA_KV_HEADS, SWA_HEAD_DIM)
    vb = v.reshape(B, nb, WINDOW, SWA_KV_HEADS, SWA_HEAD_DIM)
    kk = jnp.concatenate([jnp.concatenate([jnp.zeros_like(kb[:, :1]), kb[:, :-1]], axis=1), kb], axis=2)
    vv = jnp.concatenate([jnp.concatenate([jnp.zeros_like(vb[:, :1]), vb[:, :-1]], axis=1), vb], axis=2)
    s = jnp.einsum('bnqkgd,bnjkd->bnkgqj', qb, kk).astype(jnp.float32) * (SWA_HEAD_DIM ** -0.5)
    qi = jnp.arange(WINDOW)[:, None]
    kj = jnp.arange(2 * WINDOW)[None, :]
    delta = qi + WINDOW - kj
    band = (delta >= 0) & (delta < WINDOW)
    valid = band[None] & ((jnp.arange(nb)[:, None, None] > 0) | (kj[None] >= WINDOW))
    s = jnp.where(valid[None, :, None, None], s, -jnp.inf)
    sink = sinks.astype(jnp.float32).reshape(1, 1, SWA_KV_HEADS, SWA_GROUP, 1, 1)
    m = jnp.maximum(jnp.max(s, axis=-1, keepdims=True), sink)
    p = jnp.exp(s - m)
    attn = p / (jnp.sum(p, axis=-1, keepdims=True) + jnp.exp(sink - m))
    out = jnp.einsum('bnkgqj,bnjkd->bnqkgd', attn.astype(vv.dtype), vv)
    return out.reshape(B, S, SWA_HEADS * SWA_HEAD_DIM)


def hybrid_layer(x, cos, sin, norm1, w_in, mla_q_a_norm, mla_kv_a_norm, w_uq, w_ukv, mla_q_norm,
                 mla_k_norm, conv_w, conv_b, w_rg_a, b_rg_a, w_rg_x, b_rg_x, lru_lambda,
                 swa_q_norm, swa_k_norm, swa_sinks, w_branch, w_out, norm2, w_ff1, w_ff2):
    B, S, D = x.shape
    u = rms_norm(x, norm1) @ w_in
    c_q, c_kv, k_rope, lru_x, lru_gate, sq, sk, sv, gate_logits = jnp.split(u, _split_points(), axis=-1)
    o_a = mla_branch(c_q, c_kv, k_rope, mla_q_a_norm, mla_kv_a_norm, w_uq, w_ukv,
                     mla_q_norm, mla_k_norm, cos, sin)
    o_b = rglru_branch(lru_x, lru_gate, conv_w, conv_b, w_rg_a, b_rg_a, w_rg_x, b_rg_x, lru_lambda)
    o_c = swa_branch(sq, sk, sv, swa_q_norm, swa_k_norm, swa_sinks)
    o = jnp.stack([o_a, o_b, o_c], axis=2)
    branch = jnp.einsum('bsnc,ncd->bsnd', o, w_branch)
    g = jax.nn.sigmoid(gate_logits.astype(jnp.float32)).reshape(B, S, N_BRANCH, D).astype(x.dtype)
    x = x + jnp.sum(g * branch, axis=2) @ w_out
    h = rms_norm(x, norm2)
    return x + jnp.square(jax.nn.relu(h @ w_ff1)) @ w_ff2


def setup_inputs(seed: int = 0) -> dict:
    key = jax.random.key(seed)
    ks = jax.random.split(key, 24)

    def nrm(k, shape, scale):
        return jax.random.normal(k, shape, jnp.float32) * scale

    def gain(k, shape):
        return 1.0 + 0.02 * jax.random.normal(k, shape, jnp.float32)

    u = jax.random.uniform(ks[14], (DEPTH, LRU_WIDTH), jnp.float32, minval=0.9, maxval=0.999)
    s = u ** (1.0 / LRU_C)
    lru_lambda = jnp.log(s) - jnp.log1p(-s)
    return {
        "x": nrm(ks[0], (BATCH, SEQ, D_MODEL), 1.0),
        "norm1": gain(ks[1], (DEPTH, D_MODEL)),
        "w_in": nrm(ks[2], (DEPTH, D_MODEL, D_IN), D_MODEL ** -0.5),
        "mla_q_a_norm": gain(ks[3], (DEPTH, MLA_Q_RANK)),
        "mla_kv_a_norm": gain(ks[4], (DEPTH, MLA_KV_RANK)),
        "w_uq": nrm(ks[5], (DEPTH, MLA_Q_RANK, MLA_HEADS * MLA_QK_DIM), MLA_Q_RANK ** -0.5),
        "w_ukv": nrm(ks[6], (DEPTH, MLA_KV_RANK, MLA_HEADS * (MLA_NOPE_DIM + MLA_V_DIM)), MLA_KV_RANK ** -0.5),
        "mla_q_norm": gain(ks[7], (DEPTH, MLA_QK_DIM)),
        "mla_k_norm": gain(ks[8], (DEPTH, MLA_QK_DIM)),
        "conv_w": nrm(ks[9], (DEPTH, CONV_WIDTH, LRU_WIDTH), CONV_WIDTH ** -0.5),
        "conv_b": nrm(ks[10], (DEPTH, LRU_WIDTH), 0.02),
        "w_rg_a": nrm(ks[11], (DEPTH, LRU_BLOCKS, LRU_BLOCK_W, LRU_BLOCK_W), LRU_BLOCK_W ** -0.5),
        "b_rg_a": nrm(ks[12], (DEPTH, LRU_WIDTH), 0.1),
        "w_rg_x": nrm(ks[13], (DEPTH, LRU_BLOCKS, LRU_BLOCK_W, LRU_BLOCK_W), LRU_BLOCK_W ** -0.5),
        "b_rg_x": nrm(ks[15], (DEPTH, LRU_WIDTH), 0.1),
        "lru_lambda": lru_lambda,
        "swa_q_norm": gain(ks[16], (DEPTH, SWA_HEAD_DIM)),
        "swa_k_norm": gain(ks[17], (DEPTH, SWA_HEAD_DIM)),
        "swa_sinks": nrm(ks[18], (DEPTH, SWA_HEADS), 0.5),
        "w_branch": nrm(ks[19], (DEPTH, N_BRANCH, D_MODEL, D_MODEL), D_MODEL ** -0.5),
        "w_out": nrm(ks[20], (DEPTH, D_MODEL, D_MODEL), D_MODEL ** -0.5),
        "norm2": gain(ks[21], (DEPTH, D_MODEL)),
        "w_ff1": nrm(ks[22], (DEPTH, D_MODEL, D_FF), D_MODEL ** -0.5),
        "w_ff2": nrm(ks[23], (DEPTH, D_FF, D_MODEL), D_FF ** -0.5),
    }


def reference(x, norm1, w_in, mla_q_a_norm, mla_kv_a_norm, w_uq, w_ukv, mla_q_norm, mla_k_norm,
              conv_w, conv_b, w_rg_a, b_rg_a, w_rg_x, b_rg_x, lru_lambda, swa_q_norm, swa_k_norm,
              swa_sinks, w_branch, w_out, norm2, w_ff1, w_ff2):
    cos, sin = rope_tables(x.shape[1])
    for l in range(DEPTH):
        x = hybrid_layer(x, cos, sin, norm1[l], w_in[l], mla_q_a_norm[l], mla_kv_a_norm[l], w_uq[l],
                         w_ukv[l], mla_q_norm[l], mla_k_norm[l], conv_w[l], conv_b[l], w_rg_a[l],
                         b_rg_a[l], w_rg_x[l], b_rg_x[l], lru_lambda[l], swa_q_norm[l], swa_k_norm[l],
                         swa_sinks[l], w_branch[l], w_out[l], norm2[l], w_ff1[l], w_ff2[l])
    return x
```

```python
import functools
import math

import jax
import jax.numpy as jnp
import numpy as np
from jax import lax
from jax.experimental import pallas as pl
from jax.experimental.pallas import tpu as pltpu

F32 = jnp.float32
BF16 = jnp.bfloat16

LANES = 128

MLA_HEADS = 16
MLA_NOPE = 64
MLA_ROPE = 32
MLA_QK = MLA_NOPE + MLA_ROPE
MLA_V = 64
ROPE_THETA = 10000.0
LRU_BLOCKS = 4
CONV_WIDTH = 4
LRU_C = 8.0
SWA_HEADS = 16
SWA_KV_HEADS = 2
SWA_GROUP = SWA_HEADS // SWA_KV_HEADS
SWA_DIM = 64
WINDOW = 128
N_BRANCH = 3
EPS = 1e-6
NEG = -1e30

VMEM_LIMIT = 56 * 1024 * 1024


def _cparams(sem):
    return pltpu.CompilerParams(dimension_semantics=sem, vmem_limit_bytes=VMEM_LIMIT)


def _const_spec(shape):
    nd = len(shape)
    return pl.BlockSpec(shape, lambda *_: (0,) * nd, pipeline_mode=pl.Buffered(1))


def _rms(xf, gain):
    ms = jnp.mean(xf * xf, axis=-1, keepdims=True)
    return xf * lax.rsqrt(ms + EPS) * gain


def _sigmoid(x):
    return 1.0 / (1.0 + jnp.exp(-x))


def _in_proj_kernel(x_ref, g_ref, w_ref, *out_refs, seg_widths, chunk):
    xn = _rms(x_ref[...], g_ref[...]).astype(BF16)
    lo = 0
    for o_ref, width in zip(out_refs, seg_widths):
        for c in range(0, width, chunk):
            ce = min(c + chunk, width)
            o_ref[:, c:ce] = jnp.dot(
                xn, w_ref[:, lo + c:lo + ce], preferred_element_type=F32).astype(o_ref.dtype)
        lo += width


def _in_proj(x2d, gain, w_cat, seg_widths, tm=512):
    s, d = x2d.shape
    n = w_cat.shape[1]
    assert sum(seg_widths) == n and s % tm == 0
    return pl.pallas_call(
        functools.partial(_in_proj_kernel, seg_widths=tuple(seg_widths), chunk=1024),
        grid=(s // tm,),
        in_specs=[pl.BlockSpec((tm, d), lambda i: (i, 0)),
                  _const_spec((1, d)),
                  _const_spec((d, n))],
        out_specs=[pl.BlockSpec((tm, w), lambda i: (i, 0)) for w in seg_widths],
        out_shape=[jax.ShapeDtypeStruct((s, w), BF16) for w in seg_widths],
        compiler_params=_cparams(("parallel",)),
        name="in_proj",
    )(x2d, gain, w_cat)


def _mla_qkv_kernel(cq_ref, ckv_ref, kr_ref, cos_ref, sina_ref, sinb_ref, gqa_ref, gkva_ref,
                    wuq_ref, wuk_ref, wuv_ref, gq_ref, gk_ref, gkr_ref,
                    q_out, k_out, v_out, *, scale):
    tm = cq_ref.shape[0]
    lane = lax.broadcasted_iota(jnp.int32, (tm, LANES), 1)
    is_nope = lane < MLA_NOPE
    is_rope = (lane >= MLA_NOPE) & (lane < MLA_QK)
    cos = cos_ref[...]
    sina = sina_ref[...]
    sinb = sinb_ref[...]

    def rope(y):
        half = MLA_ROPE // 2
        return y * cos + pltpu.roll(y, half, 1) * sina + pltpu.roll(y, LANES - half, 1) * sinb

    cqn = _rms(cq_ref[...].astype(F32), gqa_ref[...]).astype(BF16)
    kvn = _rms(ckv_ref[...].astype(F32), gkva_ref[...]).astype(BF16)

    kr = kr_ref[...].astype(F32)
    kr_ms = jnp.sum(kr * kr, axis=-1, keepdims=True) * (1.0 / MLA_ROPE)
    kpe = rope(kr * lax.rsqrt(kr_ms + EPS) * gkr_ref[...])

    gq = gq_ref[...]
    gk = gk_ref[...]
    for h in range(MLA_HEADS):
        sl = slice(h * LANES, (h + 1) * LANES)
        y = jnp.dot(cqn, wuq_ref[:, sl], preferred_element_type=F32)
        y2 = y * y
        ms_n = jnp.sum(jnp.where(is_nope, y2, 0.0), axis=-1, keepdims=True) * (1.0 / MLA_NOPE)
        ms_r = jnp.sum(jnp.where(is_rope, y2, 0.0), axis=-1, keepdims=True) * (1.0 / MLA_ROPE)
        r = jnp.where(is_nope, lax.rsqrt(ms_n + EPS), lax.rsqrt(ms_r + EPS))
        q_out[h] = (rope(y * r * gq) * scale).astype(BF16)

        yk = jnp.dot(kvn, wuk_ref[:, sl], preferred_element_type=F32)
        ms_k = jnp.sum(yk * yk, axis=-1, keepdims=True) * (1.0 / MLA_NOPE)
        k_out[h] = (yk * lax.rsqrt(ms_k + EPS) * gk + kpe).astype(BF16)

    for j in range(MLA_HEADS // 2):
        sl = slice(j * LANES, (j + 1) * LANES)
        v_out[j] = jnp.dot(kvn, wuv_ref[:, sl], preferred_element_type=F32).astype(BF16)


def _mla_qkv(cq, ckv, kr, cos, sina, sinb, gqa, gkva, wuq, wuk, wuv, gq, gk, gkr, tm=256):
    s = cq.shape[0]
    h = MLA_HEADS
    row = lambda w: pl.BlockSpec((tm, w), lambda i: (i, 0))
    return pl.pallas_call(
        functools.partial(_mla_qkv_kernel, scale=MLA_QK ** -0.5),
        grid=(s // tm,),
        in_specs=[row(cq.shape[1]), row(ckv.shape[1]), row(LANES), row(LANES), row(LANES),
                  row(LANES),
                  _const_spec(gqa.shape), _const_spec(gkva.shape), _const_spec(wuq.shape),
                  _const_spec(wuk.shape), _const_spec(wuv.shape), _const_spec(gq.shape),
                  _const_spec(gk.shape), _const_spec(gkr.shape)],
        out_specs=[pl.BlockSpec((h, tm, LANES), lambda i: (0, i, 0)),
                   pl.BlockSpec((h, tm, LANES), lambda i: (0, i, 0)),
                   pl.BlockSpec((h // 2, tm, LANES), lambda i: (0, i, 0))],
        out_shape=[jax.ShapeDtypeStruct((h, s, LANES), BF16),
                   jax.ShapeDtypeStruct((h, s, LANES), BF16),
                   jax.ShapeDtypeStruct((h // 2, s, LANES), BF16)],
        compiler_params=_cparams(("parallel",)),
        name="mla_qkv",
    )(cq, ckv, kr, cos, sina, sinb, gqa, gkva, wuq, wuk, wuv, gq, gk, gkr)


def _mla_attn_kernel(qi_ref, ki_ref, q_ref, k_ref, v_ref, o_ref, m_sc, l_sc, acc_sc):
    t = pl.program_id(1)
    qi = qi_ref[t]
    ki = ki_ref[t]
    tq = q_ref.shape[1]
    tk = k_ref.shape[1]

    @pl.when(ki == 0)
    def _():
        m_sc[...] = jnp.full(m_sc.shape, NEG, F32)
        l_sc[...] = jnp.zeros(l_sc.shape, F32)
        acc_sc[...] = jnp.zeros(acc_sc.shape, F32)

    row = lax.broadcasted_iota(jnp.int32, (tq, tk), 0)
    col = lax.broadcasted_iota(jnp.int32, (tq, tk), 1)
    keep = (col <= row) | (ki < qi)
    v = v_ref[0]
    for h in range(2):
        s = lax.dot_general(q_ref[h], k_ref[h], (((1,), (1,)), ((), ())),
                            preferred_element_type=F32)
        s = jnp.where(keep, s, NEG)
        m_prev = m_sc[h]
        m_new = jnp.maximum(m_prev, jnp.max(s, axis=-1, keepdims=True))
        alpha = jnp.exp(m_prev - m_new)
        p = jnp.exp(s - m_new)
        l_sc[h] = alpha * l_sc[h] + jnp.sum(p, axis=-1, keepdims=True)
        acc_sc[h] = alpha * acc_sc[h] + jnp.dot(p.astype(BF16), v, preferred_element_type=F32)
        m_sc[h] = m_new

    @pl.when(ki == qi)
    def _():
        o0 = acc_sc[0] / l_sc[0]
        o1 = acc_sc[1] / l_sc[1]
        lane = lax.broadcasted_iota(jnp.int32, o0.shape, 1)
        o_ref[...] = jnp.where(lane < MLA_V, o0, o1).astype(o_ref.dtype)


def _mla_attn(q, k, v, tq=1024):
    h, s, _ = q.shape
    nq = s // tq
    qi_tab = np.concatenate([np.full(i + 1, i, np.int32) for i in range(nq)])
    ki_tab = np.concatenate([np.arange(i + 1, dtype=np.int32) for i in range(nq)])
    grid_spec = pltpu.PrefetchScalarGridSpec(
        num_scalar_prefetch=2,
        grid=(h // 2, len(qi_tab)),
        in_specs=[pl.BlockSpec((2, tq, LANES), lambda p, t, qi, ki: (p, qi[t], 0)),
                  pl.BlockSpec((2, tq, LANES), lambda p, t, qi, ki: (p, ki[t], 0)),
                  pl.BlockSpec((1, tq, LANES), lambda p, t, qi, ki: (p, ki[t], 0))],
        out_specs=pl.BlockSpec((tq, LANES), lambda p, t, qi, ki: (qi[t], p)),
        scratch_shapes=[pltpu.VMEM((2, tq, 1), F32), pltpu.VMEM((2, tq, 1), F32),
                        pltpu.VMEM((2, tq, LANES), F32)],
    )
    return pl.pallas_call(
        _mla_attn_kernel,
        grid_spec=grid_spec,
        out_shape=jax.ShapeDtypeStruct((s, h * MLA_V), BF16),
        compiler_params=_cparams(("parallel", "arbitrary")),
        name="mla_attn",
    )(jnp.asarray(qi_tab), jnp.asarray(ki_tab), q, k, v)


def _rglru_kernel(x_ref, gate_ref, cw_ref, cb_ref, wa_ref, ba_ref, wx_ref, bx_ref, lam_ref,
                  o_ref, xext, a_sc, b_sc, h_sc, hcar):
    i = pl.program_id(0)
    tm, width = x_ref.shape
    pad = 8

    @pl.when(i == 0)
    def _():
        xext[0:pad, :] = jnp.zeros((pad, width), F32)
        hcar[...] = jnp.zeros(hcar.shape, F32)

    @pl.when(i > 0)
    def _():
        xext[0:pad, :] = xext[tm:tm + pad, :]

    xext[pad:pad + tm, :] = x_ref[...].astype(F32)
    xc = cb_ref[...]
    for kk in range(CONV_WIDTH):
        xc = xc + cw_ref[kk:kk + 1, :] * xext[pl.ds(pad - (CONV_WIDTH - 1) + kk, tm), :]
    xcb = xc.astype(BF16)

    z = -lam_ref[...]
    softplus = jnp.maximum(z, 0.0) + jnp.log(1.0 + jnp.exp(-jnp.abs(z)))
    bw = width // LRU_BLOCKS
    for b in range(LRU_BLOCKS):
        sl = slice(b * bw, (b + 1) * bw)
        xb = xcb[:, sl]
        r = _sigmoid(jnp.dot(xb, wa_ref[b], preferred_element_type=F32) + ba_ref[:, sl])
        ig = _sigmoid(jnp.dot(xb, wx_ref[b], preferred_element_type=F32) + bx_ref[:, sl])
        a = jnp.exp(-LRU_C * r * softplus[:, sl])
        a_sc[:, sl] = a
        b_sc[:, sl] = jnp.sqrt(1.0 - a * a) * (ig * xc[:, sl])

    def step(t, h):
        h = a_sc[pl.ds(t, 1), :] * h + b_sc[pl.ds(t, 1), :]
        h_sc[pl.ds(t, 1), :] = h
        return h

    h_last = lax.fori_loop(0, tm, step, hcar[0:1, :], unroll=8)
    hcar[0:1, :] = h_last

    g = gate_ref[...].astype(F32)
    gelu = 0.5 * g * (1.0 + jnp.tanh(math.sqrt(2.0 / math.pi) * (g + 0.044715 * (g * g * g))))
    o_ref[...] = (h_sc[...] * gelu).astype(o_ref.dtype)


def _rglru(lx, lg, cw, cb, wa, ba, wx, bx, lam, tm=256):
    s, width = lx.shape
    row = pl.BlockSpec((tm, width), lambda i: (i, 0))
    return pl.pallas_call(
        _rglru_kernel,
        grid=(s // tm,),
        in_specs=[row, row, _const_spec(cw.shape), _const_spec(cb.shape), _const_spec(wa.shape),
                  _const_spec(ba.shape), _const_spec(wx.shape), _const_spec(bx.shape),
                  _const_spec(lam.shape)],
        out_specs=row,
        out_shape=jax.ShapeDtypeStruct((s, width), BF16),
        scratch_shapes=[pltpu.VMEM((tm + 8, width), F32), pltpu.VMEM((tm, width), F32),
                        pltpu.VMEM((tm, width), F32), pltpu.VMEM((tm, width), F32),
                        pltpu.VMEM((8, width), F32)],
        compiler_params=_cparams(("arbitrary",)),
        name="rglru",
    )(lx, lg, cw, cb, wa, ba, wx, bx, lam)


def _swa_kernel(sinks_ref, q_ref, kc_ref, kp_ref, vc_ref, vp_ref, gq_ref, gk_ref, o_ref, *, scale):
    n = pl.program_id(0)
    w = q_ref.shape[0]
    qi = lax.broadcasted_iota(jnp.int32, (w, 2 * w), 0)
    kj = lax.broadcasted_iota(jnp.int32, (w, 2 * w), 1)
    delta = qi + w - kj
    valid = (delta >= 0) & (delta < w) & ((n > 0) | (kj >= w))
    lane = lax.broadcasted_iota(jnp.int32, (w, LANES), 1)
    gq = gq_ref[...]
    gk = gk_ref[...]
    for kv in range(SWA_KV_HEADS):
        sl = slice(kv * LANES, (kv + 1) * LANES)
        kk = jnp.concatenate([kp_ref[:, sl], kc_ref[:, sl]], axis=0).astype(F32)
        k_ms = jnp.sum(kk * kk, axis=-1, keepdims=True) * (1.0 / SWA_DIM)
        kk = (kk * lax.rsqrt(k_ms + EPS) * gk).astype(BF16)
        vv = jnp.concatenate([vp_ref[:, sl], vc_ref[:, sl]], axis=0)
        outs = []
        for g in range(SWA_GROUP):
            head = kv * SWA_GROUP + g
            q = q_ref[:, head * LANES:(head + 1) * LANES].astype(F32)
            q_ms = jnp.sum(q * q, axis=-1, keepdims=True) * (1.0 / SWA_DIM)
            q = (q * lax.rsqrt(q_ms + EPS) * gq * scale).astype(BF16)
            s = lax.dot_general(q, kk, (((1,), (1,)), ((), ())), preferred_element_type=F32)
            s = jnp.where(valid, s, NEG)
            sink = sinks_ref[head]
            m = jnp.maximum(jnp.max(s, axis=-1, keepdims=True), sink)
            p = jnp.exp(s - m)
            denom = jnp.sum(p, axis=-1, keepdims=True) + jnp.exp(sink - m)
            attn = (p / denom).astype(BF16)
            outs.append(jnp.dot(attn, vv, preferred_element_type=F32))
        for j in range(SWA_GROUP // 2):
            pair = kv * (SWA_GROUP // 2) + j
            o_ref[:, pair * LANES:(pair + 1) * LANES] = jnp.where(
                lane < SWA_DIM, outs[2 * j], outs[2 * j + 1]).astype(o_ref.dtype)


def _swa(sinks, sq, sk, sv, gq, gk):
    s = sq.shape[0]
    w = WINDOW
    cur = lambda width: pl.BlockSpec((w, width), lambda n: (n, 0))
    prev = lambda width: pl.BlockSpec((w, width), lambda n: (jnp.maximum(n - 1, 0), 0))
    return pl.pallas_call(
        functools.partial(_swa_kernel, scale=SWA_DIM ** -0.5),
        grid=(s // w,),
        in_specs=[pl.BlockSpec(memory_space=pltpu.SMEM),
                  cur(sq.shape[1]), cur(sk.shape[1]), prev(sk.shape[1]),
                  cur(sv.shape[1]), prev(sv.shape[1]),
                  _const_spec(gq.shape), _const_spec(gk.shape)],
        out_specs=cur(SWA_HEADS * SWA_DIM),
        out_shape=jax.ShapeDtypeStruct((s, SWA_HEADS * SWA_DIM), BF16),
        compiler_params=_cparams(("parallel",)),
        name="swa",
    )(sinks, sq, sk, sk, sv, sv, gq, gk)


def _merge_kernel(x_ref, oa_ref, ob_ref, oc_ref, g_ref, wb_ref, wo_ref, y_ref):
    d = x_ref.shape[1]
    merged = None
    for b, o_ref in enumerate((oa_ref, ob_ref, oc_ref)):
        br = jnp.dot(o_ref[...], wb_ref[b], preferred_element_type=F32)
        term = _sigmoid(g_ref[:, b * d:(b + 1) * d].astype(F32)) * br
        merged = term if merged is None else merged + term
    y_ref[...] = x_ref[...] + jnp.dot(merged.astype(BF16), wo_ref[...],
                                      preferred_element_type=F32)


def _merge(x2d, oa, ob, oc, gates, wb, wo, tm=512):
    s, d = x2d.shape
    row = lambda w: pl.BlockSpec((tm, w), lambda i: (i, 0))
    return pl.pallas_call(
        _merge_kernel,
        grid=(s // tm,),
        in_specs=[row(d), row(d), row(d), row(d), row(N_BRANCH * d),
                  _const_spec(wb.shape), _const_spec(wo.shape)],
        out_specs=row(d),
        out_shape=jax.ShapeDtypeStruct((s, d), F32),
        compiler_params=_cparams(("parallel",)),
        name="merge",
    )(x2d, oa, ob, oc, gates, wb, wo)


def _ffn_kernel(x_ref, g_ref, w1_ref, w2_ref, y_ref, *, chunk):
    x = x_ref[...]
    hn = _rms(x, g_ref[...]).astype(BF16)
    acc = x
    for c in range(0, w1_ref.shape[1], chunk):
        a = jnp.maximum(jnp.dot(hn, w1_ref[:, c:c + chunk], preferred_element_type=F32), 0.0)
        acc = acc + jnp.dot((a * a).astype(BF16), w2_ref[c:c + chunk, :],
                            preferred_element_type=F32)
    y_ref[...] = acc


def _ffn(x2d, gain, w1, w2, tm=512):
    s, d = x2d.shape
    row = pl.BlockSpec((tm, d), lambda i: (i, 0))
    return pl.pallas_call(
        functools.partial(_ffn_kernel, chunk=1024),
        grid=(s // tm,),
        in_specs=[row, _const_spec(gain.shape), _const_spec(w1.shape), _const_spec(w2.shape)],
        out_specs=row,
        out_shape=jax.ShapeDtypeStruct((s, d), F32),
        compiler_params=_cparams(("parallel",)),
        name="ffn",
    )(x2d, gain, w1, w2)


def _pad_last(a, width):
    return jnp.pad(a, [(0, 0)] * (a.ndim - 1) + [(0, width - a.shape[-1])])


def _rope_tables(seq):
    half = MLA_ROPE // 2
    pos = jnp.arange(seq, dtype=F32)
    inv = ROPE_THETA ** (-jnp.arange(0, MLA_ROPE, 2, dtype=F32) / MLA_ROPE)
    ang = pos[:, None] * inv[None, :]
    cos, sin = jnp.cos(ang), jnp.sin(ang)
    zeros = jnp.zeros((seq, half), F32)
    ones = jnp.ones((seq, MLA_NOPE), F32)
    tail = jnp.zeros((seq, LANES - MLA_QK), F32)
    nope0 = jnp.zeros((seq, MLA_NOPE), F32)
    cos_t = jnp.concatenate([ones, cos, cos, tail], axis=1)
    sina_t = jnp.concatenate([nope0, zeros, sin, tail], axis=1)
    sinb_t = jnp.concatenate([nope0, -sin, zeros, tail], axis=1)
    return cos_t, sina_t, sinb_t


def _layer_weights(l, d, w_in, mla_q_a_norm, mla_kv_a_norm, w_uq, w_ukv, mla_q_norm, mla_k_norm,
                   swa_q_norm, swa_k_norm):
    q_rank = w_uq.shape[1]
    kv_rank = w_ukv.shape[1]
    lru_w = d
    sizes = (q_rank, kv_rank, MLA_ROPE, lru_w, lru_w, SWA_HEADS * SWA_DIM,
             SWA_KV_HEADS * SWA_DIM, SWA_KV_HEADS * SWA_DIM, N_BRANCH * d)
    assert sum(sizes) == w_in.shape[2]
    offs = np.cumsum((0,) + sizes)
    seg = [w_in[l][:, offs[i]:offs[i + 1]] for i in range(len(sizes))]
    w_cq, w_ckv, w_kr, w_lx, w_lg, w_sq, w_sk, w_sv, w_g = seg
    kr_slot = jnp.concatenate(
        [jnp.zeros((d, MLA_NOPE), F32), w_kr, jnp.zeros((d, LANES - MLA_QK), F32)], axis=1)
    sq_pad = _pad_last(w_sq.reshape(d, SWA_HEADS, SWA_DIM), LANES).reshape(d, SWA_HEADS * LANES)
    sk_pad = _pad_last(w_sk.reshape(d, SWA_KV_HEADS, SWA_DIM), LANES).reshape(
        d, SWA_KV_HEADS * LANES)
    sv3 = w_sv.reshape(d, SWA_KV_HEADS, SWA_DIM)
    sv_dup = jnp.concatenate([sv3, sv3], axis=-1).reshape(d, SWA_KV_HEADS * LANES)
    w_cat = jnp.concatenate([w_cq, w_ckv, kr_slot, w_lx, w_lg, sq_pad, sk_pad, sv_dup, w_g],
                            axis=1).astype(BF16)
    seg_widths = (q_rank, kv_rank, LANES, lru_w, lru_w, SWA_HEADS * LANES, SWA_KV_HEADS * LANES,
                  SWA_KV_HEADS * LANES, N_BRANCH * d)

    wuq = _pad_last(w_uq[l].reshape(q_rank, MLA_HEADS, MLA_QK), LANES).reshape(
        q_rank, MLA_HEADS * LANES).astype(BF16)
    wukv = w_ukv[l].reshape(kv_rank, MLA_HEADS, MLA_NOPE + MLA_V)
    wuk = _pad_last(wukv[..., :MLA_NOPE], LANES).reshape(kv_rank, MLA_HEADS * LANES).astype(BF16)
    wuv = wukv[..., MLA_NOPE:].reshape(kv_rank, MLA_HEADS * MLA_V).astype(BF16)
    gq = _pad_last(mla_q_norm[l][None, :], LANES)
    gk = _pad_last(mla_k_norm[l][None, :MLA_NOPE], LANES)
    gkr = _pad_last(jnp.concatenate([jnp.zeros((MLA_NOPE,), F32), mla_k_norm[l][MLA_NOPE:]])[None],
                    LANES)
    sgq = _pad_last(swa_q_norm[l][None, :], LANES)
    sgk = _pad_last(swa_k_norm[l][None, :], LANES)
    return dict(w_cat=w_cat, seg_widths=seg_widths, wuq=wuq, wuk=wuk, wuv=wuv, gq=gq, gk=gk,
                gkr=gkr, gqa=mla_q_a_norm[l][None, :], gkva=mla_kv_a_norm[l][None, :],
                sgq=sgq, sgk=sgk)


def kernel(x, norm1, w_in, mla_q_a_norm, mla_kv_a_norm, w_uq, w_ukv, mla_q_norm, mla_k_norm,
           conv_w, conv_b, w_rg_a, b_rg_a, w_rg_x, b_rg_x, lru_lambda, swa_q_norm, swa_k_norm,
           swa_sinks, w_branch, w_out, norm2, w_ff1, w_ff2):
    batch, seq, d = x.shape
    assert batch == 1
    depth = w_in.shape[0]
    cos_t, sina_t, sinb_t = _rope_tables(seq)
    h = x.reshape(seq, d)
    for l in range(depth):
        lw = _layer_weights(l, d, w_in, mla_q_a_norm, mla_kv_a_norm, w_uq, w_ukv, mla_q_norm,
                            mla_k_norm, swa_q_norm, swa_k_norm)
        cq, ckv, kr, lx, lg, sq, sk, sv, gates = _in_proj(
            h, norm1[l][None, :], lw["w_cat"], lw["seg_widths"])
        q, k, v = _mla_qkv(cq, ckv, kr, cos_t, sina_t, sinb_t, lw["gqa"], lw["gkva"], lw["wuq"],
                           lw["wuk"], lw["wuv"], lw["gq"], lw["gk"], lw["gkr"])
        o_a = _mla_attn(q, k, v)
        o_b = _rglru(lx, lg, conv_w[l], conv_b[l][None, :], w_rg_a[l].astype(BF16),
                     b_rg_a[l][None, :], w_rg_x[l].astype(BF16), b_rg_x[l][None, :],
                     lru_lambda[l][None, :])
        o_c = _swa(swa_sinks[l], sq, sk, sv, lw["sgq"], lw["sgk"])
        h = _merge(h, o_a, o_b, o_c, gates, w_branch[l].astype(BF16), w_out[l].astype(BF16))
        h = _ffn(h, norm2[l][None, :], w_ff1[l].astype(BF16), w_ff2[l].astype(BF16))
    return h.reshape(batch, seq, d)
```

```python
import functools
import math

import jax
import jax.numpy as jnp
import numpy as np
from jax import lax
from jax.experimental import pallas as pl
from jax.experimental.pallas import tpu as pltpu

F32 = jnp.float32
BF16 = jnp.bfloat16

LANES = 128

MLA_HEADS = 16
MLA_NOPE = 64
MLA_ROPE = 32
MLA_QK = MLA_NOPE + MLA_ROPE
MLA_V = 64
ROPE_THETA = 10000.0
LRU_BLOCKS = 4
CONV_WIDTH = 4
LRU_C = 8.0
SWA_HEADS = 16
SWA_KV_HEADS = 2
SWA_GROUP = SWA_HEADS // SWA_KV_HEADS
SWA_DIM = 64
WINDOW = 128
N_BRANCH = 3
EPS = 1e-6
NEG = -1e30

VMEM_LIMIT = 56 * 1024 * 1024


def _cparams(sem):
    return pltpu.CompilerParams(dimension_semantics=sem, vmem_limit_bytes=VMEM_LIMIT)


def _const_spec(shape):
    nd = len(shape)
    return pl.BlockSpec(shape, lambda *_: (0,) * nd, pipeline_mode=pl.Buffered(1))


def _rms(xf, gain):
    ms = jnp.mean(xf * xf, axis=-1, keepdims=True)
    return xf * lax.rsqrt(ms + EPS) * gain


def _sigmoid(x):
    return 1.0 / (1.0 + jnp.exp(-x))


def _in_proj_kernel(x_ref, g_ref, w_ref, *out_refs, seg_widths, chunk):
    xn = _rms(x_ref[...], g_ref[...]).astype(BF16)
    lo = 0
    for o_ref, width in zip(out_refs, seg_widths):
        for c in range(0, width, chunk):
            ce = min(c + chunk, width)
            o_ref[:, c:ce] = jnp.dot(
                xn, w_ref[:, lo + c:lo + ce], preferred_element_type=F32).astype(o_ref.dtype)
        lo += width


def _in_proj(x2d, gain, w_cat, seg_widths, tm=512):
    s, d = x2d.shape
    n = w_cat.shape[1]
    assert sum(seg_widths) == n and s % tm == 0
    return pl.pallas_call(
        functools.partial(_in_proj_kernel, seg_widths=tuple(seg_widths), chunk=1024),
        grid=(s // tm,),
        in_specs=[pl.BlockSpec((tm, d), lambda i: (i, 0)),
                  _const_spec((1, d)),
                  _const_spec((d, n))],
        out_specs=[pl.BlockSpec((tm, w), lambda i: (i, 0)) for w in seg_widths],
        out_shape=[jax.ShapeDtypeStruct((s, w), BF16) for w in seg_widths],
        compiler_params=_cparams(("parallel",)),
        name="in_proj",
    )(x2d, gain, w_cat)


def _mla_qkv_kernel(cq_ref, ckv_ref, kr_ref, cos_ref, sin_ref, gqa_ref, gkva_ref,
                    wuq_ref, wuk_ref, wuv_ref, gq_ref, gk_ref, gkr_ref,
                    q_out, kt_out, v_out, *, scale):
    tm = cq_ref.shape[0]
    lane = lax.broadcasted_iota(jnp.int32, (tm, LANES), 1)
    is_nope = lane < MLA_NOPE
    is_rope = (lane >= MLA_NOPE) & (lane < MLA_QK)
    cos = cos_ref[...]
    sin = sin_ref[...]

    def rope(y):
        return y * cos + pltpu.roll(y, LANES - MLA_ROPE // 2, 1) * sin

    cqn = _rms(cq_ref[...].astype(F32), gqa_ref[...]).astype(BF16)
    kvn = _rms(ckv_ref[...].astype(F32), gkva_ref[...]).astype(BF16)

    kr = kr_ref[...].astype(F32)
    kr_ms = jnp.sum(jnp.where(is_rope, kr * kr, 0.0), axis=-1, keepdims=True) * (1.0 / MLA_ROPE)
    kpe = rope(kr * lax.rsqrt(kr_ms + EPS) * gkr_ref[...])

    gq = gq_ref[...]
    gk = gk_ref[...]
    for g in range(MLA_HEADS // 2):
        sl2 = slice(2 * g * LANES, (2 * g + 2) * LANES)
        yq2 = jnp.dot(cqn, wuq_ref[:, sl2], preferred_element_type=F32)
        yk2 = jnp.dot(kvn, wuk_ref[:, sl2], preferred_element_type=F32)
        for i in range(2):
            h = 2 * g + i
            y = yq2[:, i * LANES:(i + 1) * LANES]
            y2 = y * y
            ms_n = jnp.sum(jnp.where(is_nope, y2, 0.0), axis=-1, keepdims=True) * (1.0 / MLA_NOPE)
            ms_r = jnp.sum(jnp.where(is_rope, y2, 0.0), axis=-1, keepdims=True) * (1.0 / MLA_ROPE)
            r = jnp.where(is_nope, lax.rsqrt(ms_n + EPS), lax.rsqrt(ms_r + EPS))
            q_out[h] = (rope(y * r * gq) * scale).astype(BF16)

            yk = yk2[:, i * LANES:(i + 1) * LANES]
            ms_k = jnp.sum(yk * yk, axis=-1, keepdims=True) * (1.0 / MLA_NOPE)
            kt_out[h] = (yk * lax.rsqrt(ms_k + EPS) * gk + kpe).T.astype(BF16)

    ones = jnp.ones((tm, LANES), BF16)
    for g in range(MLA_HEADS // 4):
        v2 = jnp.dot(kvn, wuv_ref[:, 2 * g * LANES:(2 * g + 2) * LANES],
                     preferred_element_type=F32).astype(BF16)
        for i in range(2):
            v_out[2 * g + i, :, 0:LANES] = v2[:, i * LANES:(i + 1) * LANES]
            v_out[2 * g + i, :, LANES:2 * LANES] = ones


def _mla_qkv(cq, ckv, kr, cos, sin, gqa, gkva, wuq, wuk, wuv, gq, gk, gkr, tm=256):
    s = cq.shape[0]
    h = MLA_HEADS
    row = lambda w: pl.BlockSpec((tm, w), lambda i: (i, 0))
    return pl.pallas_call(
        functools.partial(_mla_qkv_kernel, scale=MLA_QK ** -0.5 * math.log2(math.e)),
        grid=(s // tm,),
        in_specs=[row(cq.shape[1]), row(ckv.shape[1]), row(LANES), row(LANES), row(LANES),
                  _const_spec(gqa.shape), _const_spec(gkva.shape), _const_spec(wuq.shape),
                  _const_spec(wuk.shape), _const_spec(wuv.shape), _const_spec(gq.shape),
                  _const_spec(gk.shape), _const_spec(gkr.shape)],
        out_specs=[pl.BlockSpec((h, tm, LANES), lambda i: (0, i, 0)),
                   pl.BlockSpec((h, LANES, tm), lambda i: (0, 0, i)),
                   pl.BlockSpec((h // 2, tm, 2 * LANES), lambda i: (0, i, 0))],
        out_shape=[jax.ShapeDtypeStruct((h, s, LANES), BF16),
                   jax.ShapeDtypeStruct((h, LANES, s), BF16),
                   jax.ShapeDtypeStruct((h // 2, s, 2 * LANES), BF16)],
        compiler_params=_cparams(("parallel",)),
        name="mla_qkv",
    )(cq, ckv, kr, cos, sin, gqa, gkva, wuq, wuk, wuv, gq, gk, gkr)


def _attn_rows(q_ref, kt_ref, v_ref, m_sc, acc_sc, h, r, rb, diag):
    tk = kt_ref.shape[2]
    ncol = r + rb if diag else tk
    s = jnp.dot(q_ref[h, r:r + rb, :], kt_ref[h, :, 0:ncol], preferred_element_type=F32)
    tiles = [s[:, j * LANES:(j + 1) * LANES] for j in range(ncol // LANES)]
    if diag:
        rowi = lax.broadcasted_iota(jnp.int32, (rb, LANES), 0)
        coli = lax.broadcasted_iota(jnp.int32, (rb, LANES), 1)
        for jj in range(rb // LANES):
            j = r // LANES + jj
            tiles[j] = jnp.where(coli + jj * LANES <= rowi, tiles[j], NEG)
    mloc = functools.reduce(jnp.maximum, tiles)
    m_prev = m_sc[h, r:r + rb, :]
    m_new = jnp.maximum(m_prev, jnp.max(mloc, axis=-1, keepdims=True))
    alpha = jnp.exp2(m_prev - m_new)
    p = jnp.concatenate([jnp.exp2(t - m_new).astype(BF16) for t in tiles], axis=1)
    pv = jnp.dot(p, v_ref[h // 2, 0:ncol, :], preferred_element_type=F32)
    acc_sc[h, r:r + rb, :] = jnp.concatenate([alpha, alpha], axis=1) * acc_sc[h, r:r + rb, :] + pv
    m_sc[h, r:r + rb, :] = m_new


def _mla_attn_kernel(qi_ref, ki_ref, q_ref, kt_ref, v_ref, o_ref, m_sc, acc_sc, *, rb):
    t = pl.program_id(1)
    qi = qi_ref[t]
    ki = ki_ref[t]
    tq = q_ref.shape[1]

    @pl.when(ki == 0)
    def _():
        m_sc[...] = jnp.full(m_sc.shape, NEG, F32)
        acc_sc[...] = jnp.zeros(acc_sc.shape, F32)

    hp = q_ref.shape[0]

    @pl.when(ki < qi)
    def _():
        for h in range(hp):
            for r in range(0, tq, rb):
                _attn_rows(q_ref, kt_ref, v_ref, m_sc, acc_sc, h, r, rb, diag=False)

    @pl.when(ki == qi)
    def _():
        for h in range(hp):
            for r in range(0, tq, rb):
                _attn_rows(q_ref, kt_ref, v_ref, m_sc, acc_sc, h, r, rb, diag=True)
        lane = lax.broadcasted_iota(jnp.int32, (tq, LANES), 1)
        for j in range(hp // 2):
            o0 = acc_sc[2 * j, :, 0:LANES] / acc_sc[2 * j, :, LANES:2 * LANES]
            o1 = acc_sc[2 * j + 1, :, 0:LANES] / acc_sc[2 * j + 1, :, LANES:2 * LANES]
            o_ref[:, j * LANES:(j + 1) * LANES] = jnp.where(lane < MLA_V, o0, o1).astype(o_ref.dtype)


def _mla_attn(q, kt, v, tq=1024, rb=256, hp=4):
    h, s, _ = q.shape
    nq = s // tq
    qi_tab = np.concatenate([np.full(i + 1, i, np.int32) for i in range(nq)])
    ki_tab = np.concatenate([np.arange(i + 1, dtype=np.int32) for i in range(nq)])
    grid_spec = pltpu.PrefetchScalarGridSpec(
        num_scalar_prefetch=2,
        grid=(h // hp, len(qi_tab)),
        in_specs=[pl.BlockSpec((hp, tq, LANES), lambda p, t, qi, ki: (p, qi[t], 0)),
                  pl.BlockSpec((hp, LANES, tq), lambda p, t, qi, ki: (p, 0, ki[t])),
                  pl.BlockSpec((hp // 2, tq, 2 * LANES), lambda p, t, qi, ki: (p, ki[t], 0))],
        out_specs=pl.BlockSpec((tq, hp // 2 * LANES), lambda p, t, qi, ki: (qi[t], p)),
        scratch_shapes=[pltpu.VMEM((hp, tq, LANES), F32),
                        pltpu.VMEM((hp, tq, 2 * LANES), F32)],
    )
    return pl.pallas_call(
        functools.partial(_mla_attn_kernel, rb=rb),
        grid_spec=grid_spec,
        out_shape=jax.ShapeDtypeStruct((s, h * MLA_V), BF16),
        compiler_params=_cparams(("parallel", "arbitrary")),
        name="mla_attn",
    )(jnp.asarray(qi_tab), jnp.asarray(ki_tab), q, kt, v)


def _rglru_kernel(x_ref, gate_ref, cw_ref, cb_ref, wa_ref, ba_ref, wx_ref, bx_ref, lam_ref,
                  o_ref, xext, a_sc, b_sc, h_sc, hcar):
    i = pl.program_id(0)
    tm, width = x_ref.shape
    pad = 8

    @pl.when(i == 0)
    def _():
        xext[0:pad, :] = jnp.zeros((pad, width), F32)
        hcar[...] = jnp.zeros(hcar.shape, F32)

    @pl.when(i > 0)
    def _():
        xext[0:pad, :] = xext[tm:tm + pad, :]

    xext[pad:pad + tm, :] = x_ref[...].astype(F32)
    xc = cb_ref[...]
    for kk in range(CONV_WIDTH):
        xc = xc + cw_ref[kk:kk + 1, :] * xext[pl.ds(pad - (CONV_WIDTH - 1) + kk, tm), :]
    xcb = xc.astype(BF16)

    z = -lam_ref[...]
    softplus = jnp.maximum(z, 0.0) + jnp.log(1.0 + jnp.exp(-jnp.abs(z)))
    bw = width // LRU_BLOCKS
    for b in range(LRU_BLOCKS):
        sl = slice(b * bw, (b + 1) * bw)
        xb = xcb[:, sl]
        r = _sigmoid(jnp.dot(xb, wa_ref[b], preferred_element_type=F32) + ba_ref[:, sl])
        ig = _sigmoid(jnp.dot(xb, wx_ref[b], preferred_element_type=F32) + bx_ref[:, sl])
        a = jnp.exp(-LRU_C * r * softplus[:, sl])
        a_sc[:, sl] = a
        b_sc[:, sl] = jnp.sqrt(1.0 - a * a) * (ig * xc[:, sl])

    def step(t, h):
        h = a_sc[pl.ds(t, 1), :] * h + b_sc[pl.ds(t, 1), :]
        h_sc[pl.ds(t, 1), :] = h
        return h

    h_last = lax.fori_loop(0, tm, step, hcar[0:1, :], unroll=8)
    hcar[0:1, :] = h_last

    g = gate_ref[...].astype(F32)
    gelu = 0.5 * g * (1.0 + jnp.tanh(math.sqrt(2.0 / math.pi) * (g + 0.044715 * (g * g * g))))
    o_ref[...] = (h_sc[...] * gelu).astype(o_ref.dtype)


def _rglru(lx, lg, cw, cb, wa, ba, wx, bx, lam, tm=256):
    s, width = lx.shape
    row = pl.BlockSpec((tm, width), lambda i: (i, 0))
    return pl.pallas_call(
        _rglru_kernel,
        grid=(s // tm,),
        in_specs=[row, row, _const_spec(cw.shape), _const_spec(cb.shape), _const_spec(wa.shape),
                  _const_spec(ba.shape), _const_spec(wx.shape), _const_spec(bx.shape),
                  _const_spec(lam.shape)],
        out_specs=row,
        out_shape=jax.ShapeDtypeStruct((s, width), BF16),
        scratch_shapes=[pltpu.VMEM((tm + 8, width), F32), pltpu.VMEM((tm, width), F32),
                        pltpu.VMEM((tm, width), F32), pltpu.VMEM((tm, width), F32),
                        pltpu.VMEM((8, width), F32)],
        compiler_params=_cparams(("arbitrary",)),
        name="rglru",
    )(lx, lg, cw, cb, wa, ba, wx, bx, lam)


def _swa_kernel(sinks_ref, q_ref, kc_ref, kp_ref, vc_ref, vp_ref, gq_ref, gk_ref, o_ref, *, scale):
    n = pl.program_id(0)
    w = q_ref.shape[0]
    qi = lax.broadcasted_iota(jnp.int32, (w, 2 * w), 0)
    kj = lax.broadcasted_iota(jnp.int32, (w, 2 * w), 1)
    delta = qi + w - kj
    valid = (delta >= 0) & (delta < w) & ((n > 0) | (kj >= w))
    lane = lax.broadcasted_iota(jnp.int32, (w, LANES), 1)
    gq = gq_ref[...]
    gk = gk_ref[...]
    for kv in range(SWA_KV_HEADS):
        sl = slice(kv * LANES, (kv + 1) * LANES)
        kk = jnp.concatenate([kp_ref[:, sl], kc_ref[:, sl]], axis=0).astype(F32)
        k_ms = jnp.sum(kk * kk, axis=-1, keepdims=True) * (1.0 / SWA_DIM)
        kk = (kk * lax.rsqrt(k_ms + EPS) * gk).astype(BF16)
        vv = jnp.concatenate([vp_ref[:, sl], vc_ref[:, sl]], axis=0)
        outs = []
        for g in range(SWA_GROUP):
            head = kv * SWA_GROUP + g
            q = q_ref[:, head * LANES:(head + 1) * LANES].astype(F32)
            q_ms = jnp.sum(q * q, axis=-1, keepdims=True) * (1.0 / SWA_DIM)
            q = (q * lax.rsqrt(q_ms + EPS) * gq * scale).astype(BF16)
            s = lax.dot_general(q, kk, (((1,), (1,)), ((), ())), preferred_element_type=F32)
            s = jnp.where(valid, s, NEG)
            sink = sinks_ref[head]
            m = jnp.maximum(jnp.max(s, axis=-1, keepdims=True), sink)
            p = jnp.exp(s - m)
            denom = jnp.sum(p, axis=-1, keepdims=True) + jnp.exp(sink - m)
            attn = (p / denom).astype(BF16)
            outs.append(jnp.dot(attn, vv, preferred_element_type=F32))
        for j in range(SWA_GROUP // 2):
            pair = kv * (SWA_GROUP // 2) + j
            o_ref[:, pair * LANES:(pair + 1) * LANES] = jnp.where(
                lane < SWA_DIM, outs[2 * j], outs[2 * j + 1]).astype(o_ref.dtype)


def _swa(sinks, sq, sk, sv, gq, gk):
    s = sq.shape[0]
    w = WINDOW
    cur = lambda width: pl.BlockSpec((w, width), lambda n: (n, 0))
    prev = lambda width: pl.BlockSpec((w, width), lambda n: (jnp.maximum(n - 1, 0), 0))
    return pl.pallas_call(
        functools.partial(_swa_kernel, scale=SWA_DIM ** -0.5),
        grid=(s // w,),
        in_specs=[pl.BlockSpec(memory_space=pltpu.SMEM),
                  cur(sq.shape[1]), cur(sk.shape[1]), prev(sk.shape[1]),
                  cur(sv.shape[1]), prev(sv.shape[1]),
                  _const_spec(gq.shape), _const_spec(gk.shape)],
        out_specs=cur(SWA_HEADS * SWA_DIM),
        out_shape=jax.ShapeDtypeStruct((s, SWA_HEADS * SWA_DIM), BF16),
        compiler_params=_cparams(("parallel",)),
        name="swa",
    )(sinks, sq, sk, sk, sv, sv, gq, gk)


def _merge_kernel(x_ref, oa_ref, ob_ref, oc_ref, g_ref, wb_ref, wo_ref, y_ref):
    d = x_ref.shape[1]
    merged = None
    for b, o_ref in enumerate((oa_ref, ob_ref, oc_ref)):
        br = jnp.dot(o_ref[...], wb_ref[b], preferred_element_type=F32)
        term = _sigmoid(g_ref[:, b * d:(b + 1) * d].astype(F32)) * br
        merged = term if merged is None else merged + term
    y_ref[...] = x_ref[...] + jnp.dot(merged.astype(BF16), wo_ref[...],
                                      preferred_element_type=F32)


def _merge(x2d, oa, ob, oc, gates, wb, wo, tm=512):
    s, d = x2d.shape
    row = lambda w: pl.BlockSpec((tm, w), lambda i: (i, 0))
    return pl.pallas_call(
        _merge_kernel,
        grid=(s // tm,),
        in_specs=[row(d), row(d), row(d), row(d), row(N_BRANCH * d),
                  _const_spec(wb.shape), _const_spec(wo.shape)],
        out_specs=row(d),
        out_shape=jax.ShapeDtypeStruct((s, d), F32),
        compiler_params=_cparams(("parallel",)),
        name="merge",
    )(x2d, oa, ob, oc, gates, wb, wo)


def _ffn_kernel(x_ref, g_ref, w1_ref, w2_ref, y_ref, *, chunk):
    x = x_ref[...]
    hn = _rms(x, g_ref[...]).astype(BF16)
    acc = x
    for c in range(0, w1_ref.shape[1], chunk):
        a = jnp.maximum(jnp.dot(hn, w1_ref[:, c:c + chunk], preferred_element_type=F32), 0.0)
        acc = acc + jnp.dot((a * a).astype(BF16), w2_ref[c:c + chunk, :],
                            preferred_element_type=F32)
    y_ref[...] = acc


def _ffn(x2d, gain, w1, w2, tm=512):
    s, d = x2d.shape
    row = pl.BlockSpec((tm, d), lambda i: (i, 0))
    return pl.pallas_call(
        functools.partial(_ffn_kernel, chunk=1024),
        grid=(s // tm,),
        in_specs=[row, _const_spec(gain.shape), _const_spec(w1.shape), _const_spec(w2.shape)],
        out_specs=row,
        out_shape=jax.ShapeDtypeStruct((s, d), F32),
        compiler_params=_cparams(("parallel",)),
        name="ffn",
    )(x2d, gain, w1, w2)


def _pad_last(a, width):
    return jnp.pad(a, [(0, 0)] * (a.ndim - 1) + [(0, width - a.shape[-1])])


def _rope_tables(seq):
    half = MLA_ROPE // 2
    pos = jnp.arange(seq, dtype=F32)
    inv = ROPE_THETA ** (-jnp.arange(0, MLA_ROPE, 2, dtype=F32) / MLA_ROPE)
    ang = pos[:, None] * inv[None, :]
    cos, sin = jnp.cos(ang), jnp.sin(ang)
    assert cos.shape[1] == half
    ones = jnp.ones((seq, MLA_NOPE), F32)
    tail = jnp.zeros((seq, LANES - MLA_QK), F32)
    nope0 = jnp.zeros((seq, MLA_NOPE), F32)
    cos_t = jnp.concatenate([ones, cos, cos, tail], axis=1)
    sin_t = jnp.concatenate([nope0, -sin, sin, tail], axis=1)
    return cos_t, sin_t


def _rope_slot(a):
    half = MLA_ROPE // 2
    return _pad_last(jnp.concatenate([a, a[..., :half]], axis=-1), LANES - MLA_NOPE)


def _layer_weights(l, d, w_in, mla_q_a_norm, mla_kv_a_norm, w_uq, w_ukv, mla_q_norm, mla_k_norm,
                   swa_q_norm, swa_k_norm):
    q_rank = w_uq.shape[1]
    kv_rank = w_ukv.shape[1]
    lru_w = d
    sizes = (q_rank, kv_rank, MLA_ROPE, lru_w, lru_w, SWA_HEADS * SWA_DIM,
             SWA_KV_HEADS * SWA_DIM, SWA_KV_HEADS * SWA_DIM, N_BRANCH * d)
    assert sum(sizes) == w_in.shape[2]
    offs = np.cumsum((0,) + sizes)
    seg = [w_in[l][:, offs[i]:offs[i + 1]] for i in range(len(sizes))]
    w_cq, w_ckv, w_kr, w_lx, w_lg, w_sq, w_sk, w_sv, w_g = seg
    kr_slot = jnp.concatenate([jnp.zeros((d, MLA_NOPE), F32), _rope_slot(w_kr)], axis=1)
    sq_pad = _pad_last(w_sq.reshape(d, SWA_HEADS, SWA_DIM), LANES).reshape(d, SWA_HEADS * LANES)
    sk_pad = _pad_last(w_sk.reshape(d, SWA_KV_HEADS, SWA_DIM), LANES).reshape(
        d, SWA_KV_HEADS * LANES)
    sv3 = w_sv.reshape(d, SWA_KV_HEADS, SWA_DIM)
    sv_dup = jnp.concatenate([sv3, sv3], axis=-1).reshape(d, SWA_KV_HEADS * LANES)
    w_cat = jnp.concatenate([w_cq, w_ckv, kr_slot, w_lx, w_lg, sq_pad, sk_pad, sv_dup, w_g],
                            axis=1).astype(BF16)
    seg_widths = (q_rank, kv_rank, LANES, lru_w, lru_w, SWA_HEADS * LANES, SWA_KV_HEADS * LANES,
                  SWA_KV_HEADS * LANES, N_BRANCH * d)

    wuq3 = w_uq[l].reshape(q_rank, MLA_HEADS, MLA_QK)
    wuq = jnp.concatenate([wuq3[..., :MLA_NOPE], _rope_slot(wuq3[..., MLA_NOPE:])],
                          axis=-1).reshape(q_rank, MLA_HEADS * LANES).astype(BF16)
    wukv = w_ukv[l].reshape(kv_rank, MLA_HEADS, MLA_NOPE + MLA_V)
    wuk = _pad_last(wukv[..., :MLA_NOPE], LANES).reshape(kv_rank, MLA_HEADS * LANES).astype(BF16)
    wuv = wukv[..., MLA_NOPE:].reshape(kv_rank, MLA_HEADS * MLA_V).astype(BF16)
    gq = jnp.concatenate([mla_q_norm[l][:MLA_NOPE], _rope_slot(mla_q_norm[l][MLA_NOPE:])])[None]
    gk = _pad_last(mla_k_norm[l][None, :MLA_NOPE], LANES)
    gkr = jnp.concatenate([jnp.zeros((MLA_NOPE,), F32), _rope_slot(mla_k_norm[l][MLA_NOPE:])])[None]
    sgq = _pad_last(swa_q_norm[l][None, :], LANES)
    sgk = _pad_last(swa_k_norm[l][None, :], LANES)
    return dict(w_cat=w_cat, seg_widths=seg_widths, wuq=wuq, wuk=wuk, wuv=wuv, gq=gq, gk=gk,
                gkr=gkr, gqa=mla_q_a_norm[l][None, :], gkva=mla_kv_a_norm[l][None, :],
                sgq=sgq, sgk=sgk)


def kernel(x, norm1, w_in, mla_q_a_norm, mla_kv_a_norm, w_uq, w_ukv, mla_q_norm, mla_k_norm,
           conv_w, conv_b, w_rg_a, b_rg_a, w_rg_x, b_rg_x, lru_lambda, swa_q_norm, swa_k_norm,
           swa_sinks, w_branch, w_out, norm2, w_ff1, w_ff2):
    batch, seq, d = x.shape
    assert batch == 1
    depth = w_in.shape[0]
    cos_t, sin_t = _rope_tables(seq)
    h = x.reshape(seq, d)
    for l in range(depth):
        lw = _layer_weights(l, d, w_in, mla_q_a_norm, mla_kv_a_norm, w_uq, w_ukv, mla_q_norm,
                            mla_k_norm, swa_q_norm, swa_k_norm)
        cq, ckv, kr, lx, lg, sq, sk, sv, gates = _in_proj(
            h, norm1[l][None, :], lw["w_cat"], lw["seg_widths"])
        q, kt, v = _mla_qkv(cq, ckv, kr, cos_t, sin_t, lw["gqa"], lw["gkva"], lw["wuq"],
                            lw["wuk"], lw["wuv"], lw["gq"], lw["gk"], lw["gkr"])
        o_a = _mla_attn(q, kt, v)
        o_b = _rglru(lx, lg, conv_w[l], conv_b[l][None, :], w_rg_a[l].astype(BF16),
                     b_rg_a[l][None, :], w_rg_x[l].astype(BF16), b_rg_x[l][None, :],
                     lru_lambda[l][None, :])
        o_c = _swa(swa_sinks[l], sq, sk, sv, lw["sgq"], lw["sgk"])
        h = _merge(h, o_a, o_b, o_c, gates, w_branch[l].astype(BF16), w_out[l].astype(BF16))
        h = _ffn(h, norm2[l][None, :], w_ff1[l].astype(BF16), w_ff2[l].astype(BF16))
    return h.reshape(batch, seq, d)
```

```python
import functools
import math

import jax
import jax.numpy as jnp
import numpy as np
from jax import lax
from jax.experimental import pallas as pl
from jax.experimental.pallas import tpu as pltpu

F32 = jnp.float32
BF16 = jnp.bfloat16

LANES = 128

MLA_HEADS = 16
MLA_NOPE = 64
MLA_ROPE = 32
MLA_QK = MLA_NOPE + MLA_ROPE
MLA_V = 64
ROPE_THETA = 10000.0
LRU_BLOCKS = 4
CONV_WIDTH = 4
LRU_C = 8.0
SWA_HEADS = 16
SWA_KV_HEADS = 2
SWA_GROUP = SWA_HEADS // SWA_KV_HEADS
SWA_DIM = 64
WINDOW = 128
N_BRANCH = 3
EPS = 1e-6
NEG = -1e30
LOG2E = math.log2(math.e)

VMEM_LIMIT = 56 * 1024 * 1024


def _cparams(sem):
    return pltpu.CompilerParams(dimension_semantics=sem, vmem_limit_bytes=VMEM_LIMIT)


def _const_spec(shape):
    nd = len(shape)
    return pl.BlockSpec(shape, lambda *_: (0,) * nd, pipeline_mode=pl.Buffered(1))


def _rms(xf, gain):
    ms = jnp.mean(xf * xf, axis=-1, keepdims=True)
    return xf * lax.rsqrt(ms + EPS) * gain


def _sigmoid(x):
    return 1.0 / (1.0 + jnp.exp(-x))


def _in_proj_kernel(x_ref, g_ref, w_ref, *out_refs, seg_widths, chunk):
    xn = _rms(x_ref[...], g_ref[...]).astype(BF16)
    lo = 0
    for o_ref, width in zip(out_refs, seg_widths):
        for c in range(0, width, chunk):
            ce = min(c + chunk, width)
            o_ref[:, c:ce] = jnp.dot(
                xn, w_ref[:, lo + c:lo + ce], preferred_element_type=F32).astype(o_ref.dtype)
        lo += width


def _in_proj(x2d, gain, w_cat, seg_widths, tm=512):
    s, d = x2d.shape
    n = w_cat.shape[1]
    assert sum(seg_widths) == n and s % tm == 0
    return pl.pallas_call(
        functools.partial(_in_proj_kernel, seg_widths=tuple(seg_widths), chunk=1024),
        grid=(s // tm,),
        in_specs=[pl.BlockSpec((tm, d), lambda i: (i, 0)),
                  _const_spec((1, d)),
                  _const_spec((d, n))],
        out_specs=[pl.BlockSpec((tm, w), lambda i: (i, 0)) for w in seg_widths],
        out_shape=[jax.ShapeDtypeStruct((s, w), BF16) for w in seg_widths],
        compiler_params=_cparams(("parallel",)),
        name="in_proj",
    )(x2d, gain, w_cat)


def _segment_rsqrt(y, seg_ref, segt_ref, inv_n_ref):
    ss = jnp.dot((y * y).astype(BF16), seg_ref[...], preferred_element_type=F32)
    inv = lax.rsqrt(ss * inv_n_ref[...] + EPS)
    hi = inv.astype(BF16)
    lo = (inv - hi.astype(F32)).astype(BF16)
    return jnp.dot(jnp.concatenate([hi, lo], axis=1), segt_ref[...], preferred_element_type=F32)


def _mla_qkv_kernel(cq_ref, ckv_ref, kr_ref, cos_ref, sin_ref, gqa_ref, gkva_ref,
                    wuq_ref, wuk_ref, wuv_ref, gq_ref, gk_ref, gkr_ref,
                    qseg_ref, qsegt_ref, qn_ref, kseg_ref, ksegt_ref, kn_ref,
                    q_out, kt_out, v_out, *, scale):
    tm = cq_ref.shape[0]
    lane = lax.broadcasted_iota(jnp.int32, (tm, LANES), 1)
    is_rope = (lane >= MLA_NOPE) & (lane < MLA_QK)
    cos = cos_ref[...]
    sin = sin_ref[...]

    def rope(y):
        return y * cos + pltpu.roll(y, LANES - MLA_ROPE // 2, 1) * sin

    cqn = _rms(cq_ref[...].astype(F32), gqa_ref[...]).astype(BF16)
    kvn = _rms(ckv_ref[...].astype(F32), gkva_ref[...]).astype(BF16)

    kr = kr_ref[...].astype(F32)
    kr_ms = jnp.sum(jnp.where(is_rope, kr * kr, 0.0), axis=-1, keepdims=True) * (1.0 / MLA_ROPE)
    kpe = rope(kr * lax.rsqrt(kr_ms + EPS) * gkr_ref[...])

    gq = gq_ref[...] * scale
    gk = gk_ref[...]
    yq = jnp.dot(cqn, wuq_ref[...], preferred_element_type=F32)
    yq = yq * _segment_rsqrt(yq, qseg_ref, qsegt_ref, qn_ref)
    yk = jnp.dot(kvn, wuk_ref[...], preferred_element_type=F32)
    yk = yk * _segment_rsqrt(yk, kseg_ref, ksegt_ref, kn_ref)
    for h in range(MLA_HEADS):
        sl = slice(h * LANES, (h + 1) * LANES)
        q_out[h] = rope(yq[:, sl] * gq).astype(BF16)
        kt_out[h] = (yk[:, sl] * gk + kpe).T.astype(BF16)

    ones = jnp.ones((tm, LANES), BF16)
    for g in range(MLA_HEADS // 4):
        v2 = jnp.dot(kvn, wuv_ref[:, 2 * g * LANES:(2 * g + 2) * LANES],
                     preferred_element_type=F32).astype(BF16)
        for i in range(2):
            v_out[2 * g + i, :, 0:LANES] = v2[:, i * LANES:(i + 1) * LANES]
            v_out[2 * g + i, :, LANES:2 * LANES] = ones


def _segment_indicators(n_slots, segments):
    assert n_slots * len(segments) <= LANES
    seg = np.zeros((n_slots * LANES, LANES), np.float32)
    segt = np.zeros((LANES, n_slots * LANES), np.float32)
    inv_n = np.zeros((1, LANES), np.float32)
    col = 0
    for slot in range(n_slots):
        for start, n_sum, n_spread in segments:
            lo = slot * LANES + start
            seg[lo:lo + n_sum, col] = 1.0
            segt[col, lo:lo + n_spread] = 1.0
            inv_n[0, col] = 1.0 / n_sum
            col += 1
    return (jnp.asarray(seg, BF16), jnp.asarray(np.concatenate([segt, segt], axis=0), BF16),
            jnp.asarray(inv_n))


def _mla_qkv(cq, ckv, kr, cos, sin, gqa, gkva, wuq, wuk, wuv, gq, gk, gkr, tm=256):
    s = cq.shape[0]
    h = MLA_HEADS
    row = lambda w: pl.BlockSpec((tm, w), lambda i: (i, 0))
    qind = _segment_indicators(h, [(0, MLA_NOPE, MLA_NOPE),
                                   (MLA_NOPE, MLA_ROPE, MLA_ROPE + MLA_ROPE // 2)])
    kind = _segment_indicators(h, [(0, MLA_NOPE, MLA_NOPE)])
    return pl.pallas_call(
        functools.partial(_mla_qkv_kernel, scale=MLA_QK ** -0.5 * LOG2E),
        grid=(s // tm,),
        in_specs=[row(cq.shape[1]), row(ckv.shape[1]), row(LANES), row(LANES), row(LANES),
                  _const_spec(gqa.shape), _const_spec(gkva.shape), _const_spec(wuq.shape),
                  _const_spec(wuk.shape), _const_spec(wuv.shape), _const_spec(gq.shape),
                  _const_spec(gk.shape), _const_spec(gkr.shape)]
                 + [_const_spec(a.shape) for a in qind + kind],
        out_specs=[pl.BlockSpec((h, tm, LANES), lambda i: (0, i, 0)),
                   pl.BlockSpec((h, LANES, tm), lambda i: (0, 0, i)),
                   pl.BlockSpec((h // 2, tm, 2 * LANES), lambda i: (0, i, 0))],
        out_shape=[jax.ShapeDtypeStruct((h, s, LANES), BF16),
                   jax.ShapeDtypeStruct((h, LANES, s), BF16),
                   jax.ShapeDtypeStruct((h // 2, s, 2 * LANES), BF16)],
        compiler_params=_cparams(("parallel",)),
        name="mla_qkv",
    )(cq, ckv, kr, cos, sin, gqa, gkva, wuq, wuk, wuv, gq, gk, gkr, *qind, *kind)


def _attn_rows(q_ref, kt_ref, v_ref, m_sc, acc_sc, h, r, rb, diag):
    tk = kt_ref.shape[2]
    ncol = r + rb if diag else tk
    s = jnp.dot(q_ref[h, r:r + rb, :], kt_ref[h, :, 0:ncol], preferred_element_type=F32)
    tiles = [s[:, j * LANES:(j + 1) * LANES] for j in range(ncol // LANES)]
    if diag:
        rowi = lax.broadcasted_iota(jnp.int32, (rb, LANES), 0)
        coli = lax.broadcasted_iota(jnp.int32, (rb, LANES), 1)
        for jj in range(rb // LANES):
            j = r // LANES + jj
            tiles[j] = jnp.where(coli + jj * LANES <= rowi, tiles[j], NEG)
    mloc = functools.reduce(jnp.maximum, tiles)
    m_prev = m_sc[h, r:r + rb, :]
    m_new = jnp.maximum(m_prev, jnp.max(mloc, axis=-1, keepdims=True))
    alpha = jnp.exp2(m_prev - m_new)
    p = jnp.concatenate([jnp.exp2(t - m_new).astype(BF16) for t in tiles], axis=1)
    pv = jnp.dot(p, v_ref[h // 2, 0:ncol, :], preferred_element_type=F32)
    acc_sc[h, r:r + rb, :] = jnp.concatenate([alpha, alpha], axis=1) * acc_sc[h, r:r + rb, :] + pv
    m_sc[h, r:r + rb, :] = m_new


def _mla_attn_kernel(qi_ref, ki_ref, q_ref, kt_ref, v_ref, o_ref, m_sc, acc_sc, *, rb):
    t = pl.program_id(1)
    qi = qi_ref[t]
    ki = ki_ref[t]
    tq = q_ref.shape[1]

    @pl.when(ki == 0)
    def _():
        m_sc[...] = jnp.full(m_sc.shape, NEG, F32)
        acc_sc[...] = jnp.zeros(acc_sc.shape, F32)

    hp = q_ref.shape[0]

    @pl.when(ki < qi)
    def _():
        for h in range(hp):
            for r in range(0, tq, rb):
                _attn_rows(q_ref, kt_ref, v_ref, m_sc, acc_sc, h, r, rb, diag=False)

    @pl.when(ki == qi)
    def _():
        for h in range(hp):
            for r in range(0, tq, rb):
                _attn_rows(q_ref, kt_ref, v_ref, m_sc, acc_sc, h, r, rb, diag=True)
        lane = lax.broadcasted_iota(jnp.int32, (tq, LANES), 1)
        for j in range(hp // 2):
            o0 = acc_sc[2 * j, :, 0:LANES] / acc_sc[2 * j, :, LANES:2 * LANES]
            o1 = acc_sc[2 * j + 1, :, 0:LANES] / acc_sc[2 * j + 1, :, LANES:2 * LANES]
            o_ref[:, j * LANES:(j + 1) * LANES] = jnp.where(lane < MLA_V, o0, o1).astype(o_ref.dtype)


def _mla_attn(q, kt, v, tq=1024, rb=512, hp=8):
    h, s, _ = q.shape
    nq = s // tq
    qi_tab = np.concatenate([np.full(i + 1, i, np.int32) for i in range(nq)])
    ki_tab = np.concatenate([np.arange(i + 1, dtype=np.int32) for i in range(nq)])
    grid_spec = pltpu.PrefetchScalarGridSpec(
        num_scalar_prefetch=2,
        grid=(h // hp, len(qi_tab)),
        in_specs=[pl.BlockSpec((hp, tq, LANES), lambda p, t, qi, ki: (p, qi[t], 0)),
                  pl.BlockSpec((hp, LANES, tq), lambda p, t, qi, ki: (p, 0, ki[t])),
                  pl.BlockSpec((hp // 2, tq, 2 * LANES), lambda p, t, qi, ki: (p, ki[t], 0))],
        out_specs=pl.BlockSpec((tq, hp // 2 * LANES), lambda p, t, qi, ki: (qi[t], p)),
        scratch_shapes=[pltpu.VMEM((hp, tq, LANES), F32),
                        pltpu.VMEM((hp, tq, 2 * LANES), F32)],
    )
    return pl.pallas_call(
        functools.partial(_mla_attn_kernel, rb=rb),
        grid_spec=grid_spec,
        out_shape=jax.ShapeDtypeStruct((s, h * MLA_V), BF16),
        compiler_params=_cparams(("parallel", "arbitrary")),
        name="mla_attn",
    )(jnp.asarray(qi_tab), jnp.asarray(ki_tab), q, kt, v)


def _rglru_kernel(x_ref, gate_ref, cw_ref, cb_ref, wa_ref, ba_ref, wx_ref, bx_ref, lam_ref,
                  o_ref, xext, a_sc, b_sc, h_sc, hcar):
    i = pl.program_id(0)
    tm, width = x_ref.shape
    pad = 8

    @pl.when(i == 0)
    def _():
        xext[0:pad, :] = jnp.zeros((pad, width), F32)
        hcar[...] = jnp.zeros(hcar.shape, F32)

    @pl.when(i > 0)
    def _():
        xext[0:pad, :] = xext[tm:tm + pad, :]

    xext[pad:pad + tm, :] = x_ref[...].astype(F32)
    xc = cb_ref[...]
    for kk in range(CONV_WIDTH):
        xc = xc + cw_ref[kk:kk + 1, :] * xext[pl.ds(pad - (CONV_WIDTH - 1) + kk, tm), :]
    xcb = xc.astype(BF16)

    z = -lam_ref[...]
    softplus = jnp.maximum(z, 0.0) + jnp.log(1.0 + jnp.exp(-jnp.abs(z)))
    bw = width // LRU_BLOCKS
    for b in range(LRU_BLOCKS):
        sl = slice(b * bw, (b + 1) * bw)
        xb = xcb[:, sl]
        r = _sigmoid(jnp.dot(xb, wa_ref[b], preferred_element_type=F32) + ba_ref[:, sl])
        ig = _sigmoid(jnp.dot(xb, wx_ref[b], preferred_element_type=F32) + bx_ref[:, sl])
        a = jnp.exp(-LRU_C * r * softplus[:, sl])
        a_sc[:, sl] = a
        b_sc[:, sl] = jnp.sqrt(1.0 - a * a) * (ig * xc[:, sl])

    def step(t, h):
        h = a_sc[pl.ds(t, 1), :] * h + b_sc[pl.ds(t, 1), :]
        h_sc[pl.ds(t, 1), :] = h
        return h

    h_last = lax.fori_loop(0, tm, step, hcar[0:1, :], unroll=8)
    hcar[0:1, :] = h_last

    g = gate_ref[...].astype(F32)
    gelu = 0.5 * g * (1.0 + jnp.tanh(math.sqrt(2.0 / math.pi) * (g + 0.044715 * (g * g * g))))
    o_ref[...] = (h_sc[...] * gelu).astype(o_ref.dtype)


def _rglru(lx, lg, cw, cb, wa, ba, wx, bx, lam, tm=256):
    s, width = lx.shape
    row = pl.BlockSpec((tm, width), lambda i: (i, 0))
    return pl.pallas_call(
        _rglru_kernel,
        grid=(s // tm,),
        in_specs=[row, row, _const_spec(cw.shape), _const_spec(cb.shape), _const_spec(wa.shape),
                  _const_spec(ba.shape), _const_spec(wx.shape), _const_spec(bx.shape),
                  _const_spec(lam.shape)],
        out_specs=row,
        out_shape=jax.ShapeDtypeStruct((s, width), BF16),
        scratch_shapes=[pltpu.VMEM((tm + 8, width), F32), pltpu.VMEM((tm, width), F32),
                        pltpu.VMEM((tm, width), F32), pltpu.VMEM((tm, width), F32),
                        pltpu.VMEM((8, width), F32)],
        compiler_params=_cparams(("arbitrary",)),
        name="rglru",
    )(lx, lg, cw, cb, wa, ba, wx, bx, lam)


def _swa_kernel(sinks_ref, q_ref, kc_ref, kp_ref, vc_ref, vp_ref, gq_ref, gk_ref, o_ref, *, scale):
    n = pl.program_id(0)
    w = q_ref.shape[0]
    qi = lax.broadcasted_iota(jnp.int32, (w, 2 * w), 0)
    kj = lax.broadcasted_iota(jnp.int32, (w, 2 * w), 1)
    delta = qi + w - kj
    valid = (delta >= 0) & (delta < w) & ((n > 0) | (kj >= w))
    lane = lax.broadcasted_iota(jnp.int32, (w, LANES), 1)
    gq = gq_ref[...] * scale
    gk = gk_ref[...]
    ones = jnp.ones((2 * w, LANES), BF16)
    for kv in range(SWA_KV_HEADS):
        sl = slice(kv * LANES, (kv + 1) * LANES)
        kk = jnp.concatenate([kp_ref[:, sl], kc_ref[:, sl]], axis=0).astype(F32)
        k_ms = jnp.sum(kk * kk, axis=-1, keepdims=True) * (1.0 / SWA_DIM)
        kk = (kk * lax.rsqrt(k_ms + EPS) * gk).astype(BF16)
        vv = jnp.concatenate(
            [jnp.concatenate([vp_ref[:, sl], vc_ref[:, sl]], axis=0), ones], axis=1)
        outs = []
        for g in range(SWA_GROUP):
            head = kv * SWA_GROUP + g
            q = q_ref[:, head * LANES:(head + 1) * LANES].astype(F32)
            q_ms = jnp.sum(q * q, axis=-1, keepdims=True) * (1.0 / SWA_DIM)
            q = (q * lax.rsqrt(q_ms + EPS) * gq).astype(BF16)
            s = lax.dot_general(q, kk, (((1,), (1,)), ((), ())), preferred_element_type=F32)
            s = jnp.where(valid, s, NEG)
            sink = sinks_ref[head]
            m = jnp.maximum(jnp.max(s, axis=-1, keepdims=True), sink)
            pv = jnp.dot(jnp.exp2(s - m).astype(BF16), vv, preferred_element_type=F32)
            outs.append(pv[:, 0:LANES] / (pv[:, LANES:2 * LANES] + jnp.exp2(sink - m)))
        for j in range(SWA_GROUP // 2):
            pair = kv * (SWA_GROUP // 2) + j
            o_ref[:, pair * LANES:(pair + 1) * LANES] = jnp.where(
                lane < SWA_DIM, outs[2 * j], outs[2 * j + 1]).astype(o_ref.dtype)


def _swa(sinks, sq, sk, sv, gq, gk):
    s = sq.shape[0]
    w = WINDOW
    cur = lambda width: pl.BlockSpec((w, width), lambda n: (n, 0))
    prev = lambda width: pl.BlockSpec((w, width), lambda n: (jnp.maximum(n - 1, 0), 0))
    return pl.pallas_call(
        functools.partial(_swa_kernel, scale=SWA_DIM ** -0.5 * LOG2E),
        grid=(s // w,),
        in_specs=[pl.BlockSpec(memory_space=pltpu.SMEM),
                  cur(sq.shape[1]), cur(sk.shape[1]), prev(sk.shape[1]),
                  cur(sv.shape[1]), prev(sv.shape[1]),
                  _const_spec(gq.shape), _const_spec(gk.shape)],
        out_specs=cur(SWA_HEADS * SWA_DIM),
        out_shape=jax.ShapeDtypeStruct((s, SWA_HEADS * SWA_DIM), BF16),
        compiler_params=_cparams(("parallel",)),
        name="swa",
    )(sinks, sq, sk, sk, sv, sv, gq, gk)


def _merge_kernel(x_ref, oa_ref, ob_ref, oc_ref, g_ref, wb_ref, wo_ref, y_ref):
    d = x_ref.shape[1]
    merged = None
    for b, o_ref in enumerate((oa_ref, ob_ref, oc_ref)):
        br = jnp.dot(o_ref[...], wb_ref[b], preferred_element_type=F32)
        term = _sigmoid(g_ref[:, b * d:(b + 1) * d].astype(F32)) * br
        merged = term if merged is None else merged + term
    y_ref[...] = x_ref[...] + jnp.dot(merged.astype(BF16), wo_ref[...],
                                      preferred_element_type=F32)


def _merge(x2d, oa, ob, oc, gates, wb, wo, tm=512):
    s, d = x2d.shape
    row = lambda w: pl.BlockSpec((tm, w), lambda i: (i, 0))
    return pl.pallas_call(
        _merge_kernel,
        grid=(s // tm,),
        in_specs=[row(d), row(d), row(d), row(d), row(N_BRANCH * d),
                  _const_spec(wb.shape), _const_spec(wo.shape)],
        out_specs=row(d),
        out_shape=jax.ShapeDtypeStruct((s, d), F32),
        compiler_params=_cparams(("parallel",)),
        name="merge",
    )(x2d, oa, ob, oc, gates, wb, wo)


def _ffn_kernel(x_ref, g_ref, w1_ref, w2_ref, y_ref, *, chunk):
    x = x_ref[...]
    hn = _rms(x, g_ref[...]).astype(BF16)
    acc = x
    for c in range(0, w1_ref.shape[1], chunk):
        a = jnp.maximum(jnp.dot(hn, w1_ref[:, c:c + chunk], preferred_element_type=F32), 0.0)
        acc = acc + jnp.dot((a * a).astype(BF16), w2_ref[c:c + chunk, :],
                            preferred_element_type=F32)
    y_ref[...] = acc


def _ffn(x2d, gain, w1, w2, tm=512):
    s, d = x2d.shape
    row = pl.BlockSpec((tm, d), lambda i: (i, 0))
    return pl.pallas_call(
        functools.partial(_ffn_kernel, chunk=1024),
        grid=(s // tm,),
        in_specs=[row, _const_spec(gain.shape), _const_spec(w1.shape), _const_spec(w2.shape)],
        out_specs=row,
        out_shape=jax.ShapeDtypeStruct((s, d), F32),
        compiler_params=_cparams(("parallel",)),
        name="ffn",
    )(x2d, gain, w1, w2)


def _pad_last(a, width):
    return jnp.pad(a, [(0, 0)] * (a.ndim - 1) + [(0, width - a.shape[-1])])


def _rope_tables(seq):
    half = MLA_ROPE // 2
    pos = jnp.arange(seq, dtype=F32)
    inv = ROPE_THETA ** (-jnp.arange(0, MLA_ROPE, 2, dtype=F32) / MLA_ROPE)
    ang = pos[:, None] * inv[None, :]
    cos, sin = jnp.cos(ang), jnp.sin(ang)
    assert cos.shape[1] == half
    ones = jnp.ones((seq, MLA_NOPE), F32)
    tail = jnp.zeros((seq, LANES - MLA_QK), F32)
    nope0 = jnp.zeros((seq, MLA_NOPE), F32)
    cos_t = jnp.concatenate([ones, cos, cos, tail], axis=1)
    sin_t = jnp.concatenate([nope0, -sin, sin, tail], axis=1)
    return cos_t, sin_t


def _rope_slot(a):
    half = MLA_ROPE // 2
    return _pad_last(jnp.concatenate([a, a[..., :half]], axis=-1), LANES - MLA_NOPE)


def _layer_weights(l, d, w_in, mla_q_a_norm, mla_kv_a_norm, w_uq, w_ukv, mla_q_norm, mla_k_norm,
                   swa_q_norm, swa_k_norm):
    q_rank = w_uq.shape[1]
    kv_rank = w_ukv.shape[1]
    lru_w = d
    sizes = (q_rank, kv_rank, MLA_ROPE, lru_w, lru_w, SWA_HEADS * SWA_DIM,
             SWA_KV_HEADS * SWA_DIM, SWA_KV_HEADS * SWA_DIM, N_BRANCH * d)
    assert sum(sizes) == w_in.shape[2]
    offs = np.cumsum((0,) + sizes)
    seg = [w_in[l][:, offs[i]:offs[i + 1]] for i in range(len(sizes))]
    w_cq, w_ckv, w_kr, w_lx, w_lg, w_sq, w_sk, w_sv, w_g = seg
    kr_slot = jnp.concatenate([jnp.zeros((d, MLA_NOPE), F32), _rope_slot(w_kr)], axis=1)
    sq_pad = _pad_last(w_sq.reshape(d, SWA_HEADS, SWA_DIM), LANES).reshape(d, SWA_HEADS * LANES)
    sk_pad = _pad_last(w_sk.reshape(d, SWA_KV_HEADS, SWA_DIM), LANES).reshape(
        d, SWA_KV_HEADS * LANES)
    sv3 = w_sv.reshape(d, SWA_KV_HEADS, SWA_DIM)
    sv_dup = jnp.concatenate([sv3, sv3], axis=-1).reshape(d, SWA_KV_HEADS * LANES)
    w_cat = jnp.concatenate([w_cq, w_ckv, kr_slot, w_lx, w_lg, sq_pad, sk_pad, sv_dup, w_g],
                            axis=1).astype(BF16)
    seg_widths = (q_rank, kv_rank, LANES, lru_w, lru_w, SWA_HEADS * LANES, SWA_KV_HEADS * LANES,
                  SWA_KV_HEADS * LANES, N_BRANCH * d)

    wuq3 = w_uq[l].reshape(q_rank, MLA_HEADS, MLA_QK)
    wuq = jnp.concatenate([wuq3[..., :MLA_NOPE], _rope_slot(wuq3[..., MLA_NOPE:])],
                          axis=-1).reshape(q_rank, MLA_HEADS * LANES).astype(BF16)
    wukv = w_ukv[l].reshape(kv_rank, MLA_HEADS, MLA_NOPE + MLA_V)
    wuk = _pad_last(wukv[..., :MLA_NOPE], LANES).reshape(kv_rank, MLA_HEADS * LANES).astype(BF16)
    wuv = wukv[..., MLA_NOPE:].reshape(kv_rank, MLA_HEADS * MLA_V).astype(BF16)
    gq = jnp.concatenate([mla_q_norm[l][:MLA_NOPE], _rope_slot(mla_q_norm[l][MLA_NOPE:])])[None]
    gk = _pad_last(mla_k_norm[l][None, :MLA_NOPE], LANES)
    gkr = jnp.concatenate([jnp.zeros((MLA_NOPE,), F32), _rope_slot(mla_k_norm[l][MLA_NOPE:])])[None]
    sgq = _pad_last(swa_q_norm[l][None, :], LANES)
    sgk = _pad_last(swa_k_norm[l][None, :], LANES)
    return dict(w_cat=w_cat, seg_widths=seg_widths, wuq=wuq, wuk=wuk, wuv=wuv, gq=gq, gk=gk,
                gkr=gkr, gqa=mla_q_a_norm[l][None, :], gkva=mla_kv_a_norm[l][None, :],
                sgq=sgq, sgk=sgk)


def kernel(x, norm1, w_in, mla_q_a_norm, mla_kv_a_norm, w_uq, w_ukv, mla_q_norm, mla_k_norm,
           conv_w, conv_b, w_rg_a, b_rg_a, w_rg_x, b_rg_x, lru_lambda, swa_q_norm, swa_k_norm,
           swa_sinks, w_branch, w_out, norm2, w_ff1, w_ff2):
    batch, seq, d = x.shape
    assert batch == 1
    depth = w_in.shape[0]
    cos_t, sin_t = _rope_tables(seq)
    h = x.reshape(seq, d)
    for l in range(depth):
        lw = _layer_weights(l, d, w_in, mla_q_a_norm, mla_kv_a_norm, w_uq, w_ukv, mla_q_norm,
                            mla_k_norm, swa_q_norm, swa_k_norm)
        cq, ckv, kr, lx, lg, sq, sk, sv, gates = _in_proj(
            h, norm1[l][None, :], lw["w_cat"], lw["seg_widths"])
        q, kt, v = _mla_qkv(cq, ckv, kr, cos_t, sin_t, lw["gqa"], lw["gkva"], lw["wuq"],
                            lw["wuk"], lw["wuv"], lw["gq"], lw["gk"], lw["gkr"])
        o_a = _mla_attn(q, kt, v)
        o_b = _rglru(lx, lg, conv_w[l], conv_b[l][None, :], w_rg_a[l].astype(BF16),
                     b_rg_a[l][None, :], w_rg_x[l].astype(BF16), b_rg_x[l][None, :],
                     lru_lambda[l][None, :])
        o_c = _swa(swa_sinks[l] * LOG2E, sq, sk, sv, lw["sgq"], lw["sgk"])
        h = _merge(h, o_a, o_b, o_c, gates, w_branch[l].astype(BF16), w_out[l].astype(BF16))
        h = _ffn(h, norm2[l][None, :], w_ff1[l].astype(BF16), w_ff2[l].astype(BF16))
    return h.reshape(batch, seq, d)
```

```python
import functools
import math

import jax
import jax.numpy as jnp
import numpy as np
from jax import lax
from jax.experimental import pallas as pl
from jax.experimental.pallas import tpu as pltpu

F32 = jnp.float32
BF16 = jnp.bfloat16

LANES = 128

MLA_HEADS = 16
MLA_NOPE = 64
MLA_ROPE = 32
MLA_QK = MLA_NOPE + MLA_ROPE
MLA_V = 64
ROPE_THETA = 10000.0
LRU_BLOCKS = 4
CONV_WIDTH = 4
LRU_C = 8.0
SWA_HEADS = 16
SWA_KV_HEADS = 2
SWA_GROUP = SWA_HEADS // SWA_KV_HEADS
SWA_DIM = 64
WINDOW = 128
N_BRANCH = 3
EPS = 1e-6
NEG = -1e30
LOG2E = math.log2(math.e)

VMEM_LIMIT = 56 * 1024 * 1024


def _cparams(sem):
    return pltpu.CompilerParams(dimension_semantics=sem, vmem_limit_bytes=VMEM_LIMIT)


def _const_spec(shape):
    nd = len(shape)
    return pl.BlockSpec(shape, lambda *_: (0,) * nd, pipeline_mode=pl.Buffered(1))


def _rms(xf, gain):
    ms = jnp.mean(xf * xf, axis=-1, keepdims=True)
    return xf * lax.rsqrt(ms + EPS) * gain


def _sigmoid(x):
    return 1.0 / (1.0 + jnp.exp2(x * (-LOG2E)))


def _in_proj_kernel(x_ref, g_ref, w_ref, *out_refs, seg_widths, chunk):
    xn = _rms(x_ref[...], g_ref[...]).astype(BF16)
    lo = 0
    for o_ref, width in zip(out_refs, seg_widths):
        for c in range(0, width, chunk):
            ce = min(c + chunk, width)
            o_ref[:, c:ce] = jnp.dot(
                xn, w_ref[:, lo + c:lo + ce], preferred_element_type=F32).astype(o_ref.dtype)
        lo += width


def _in_proj(x2d, gain, w_cat, seg_widths, tm=512):
    s, d = x2d.shape
    n = w_cat.shape[1]
    assert sum(seg_widths) == n and s % tm == 0
    return pl.pallas_call(
        functools.partial(_in_proj_kernel, seg_widths=tuple(seg_widths), chunk=1024),
        grid=(s // tm,),
        in_specs=[pl.BlockSpec((tm, d), lambda i: (i, 0)),
                  _const_spec((1, d)),
                  _const_spec((d, n))],
        out_specs=[pl.BlockSpec((tm, w), lambda i: (i, 0)) for w in seg_widths],
        out_shape=[jax.ShapeDtypeStruct((s, w), BF16) for w in seg_widths],
        compiler_params=_cparams(("parallel",)),
        name="in_proj",
    )(x2d, gain, w_cat)


def _segment_rsqrt(y, seg_ref, segt_ref, inv_n_ref):
    ss = jnp.dot((y * y).astype(BF16), seg_ref[...], preferred_element_type=F32)
    inv = lax.rsqrt(ss * inv_n_ref[...] + EPS)
    hi = inv.astype(BF16)
    lo = (inv - hi.astype(F32)).astype(BF16)
    return jnp.dot(jnp.concatenate([hi, lo], axis=1), segt_ref[...], preferred_element_type=F32)


def _mla_qkv_kernel(cq_ref, ckv_ref, kr_ref, cos_ref, sin_ref, gqa_ref, gkva_ref,
                    wuq_ref, wuk_ref, wuv_ref, gq_ref, gk_ref, gkr_ref,
                    qseg_ref, qsegt_ref, qn_ref, kseg_ref, ksegt_ref, kn_ref,
                    q_out, kt_out, v_out, *, scale):
    tm = cq_ref.shape[0]
    lane = lax.broadcasted_iota(jnp.int32, (tm, LANES), 1)
    is_rope = (lane >= MLA_NOPE) & (lane < MLA_QK)
    cos = cos_ref[...]
    sin = sin_ref[...]

    def rope(y):
        return y * cos + pltpu.roll(y, LANES - MLA_ROPE // 2, 1) * sin

    cqn = _rms(cq_ref[...].astype(F32), gqa_ref[...]).astype(BF16)
    kvn = _rms(ckv_ref[...].astype(F32), gkva_ref[...]).astype(BF16)

    kr = kr_ref[...].astype(F32)
    kr_ms = jnp.sum(jnp.where(is_rope, kr * kr, 0.0), axis=-1, keepdims=True) * (1.0 / MLA_ROPE)
    kpe = rope(kr * lax.rsqrt(kr_ms + EPS) * gkr_ref[...])

    gq = gq_ref[...] * scale
    gk = gk_ref[...]
    yq = jnp.dot(cqn, wuq_ref[...], preferred_element_type=F32)
    yq = yq * _segment_rsqrt(yq, qseg_ref, qsegt_ref, qn_ref)
    yk = jnp.dot(kvn, wuk_ref[...], preferred_element_type=F32)
    yk = yk * _segment_rsqrt(yk, kseg_ref, ksegt_ref, kn_ref)
    for h in range(MLA_HEADS):
        sl = slice(h * LANES, (h + 1) * LANES)
        q_out[h] = rope(yq[:, sl] * gq).astype(BF16)
        kt_out[h] = (yk[:, sl] * gk + kpe).T.astype(BF16)

    ones = jnp.ones((tm, LANES), BF16)
    for g in range(MLA_HEADS // 4):
        v2 = jnp.dot(kvn, wuv_ref[:, 2 * g * LANES:(2 * g + 2) * LANES],
                     preferred_element_type=F32).astype(BF16)
        for i in range(2):
            v_out[2 * g + i, :, 0:LANES] = v2[:, i * LANES:(i + 1) * LANES]
            v_out[2 * g + i, :, LANES:2 * LANES] = ones


def _segment_indicators(n_slots, segments):
    assert n_slots * len(segments) <= LANES
    seg = np.zeros((n_slots * LANES, LANES), np.float32)
    segt = np.zeros((LANES, n_slots * LANES), np.float32)
    inv_n = np.zeros((1, LANES), np.float32)
    col = 0
    for slot in range(n_slots):
        for start, n_sum, n_spread in segments:
            lo = slot * LANES + start
            seg[lo:lo + n_sum, col] = 1.0
            segt[col, lo:lo + n_spread] = 1.0
            inv_n[0, col] = 1.0 / n_sum
            col += 1
    return (jnp.asarray(seg, BF16), jnp.asarray(np.concatenate([segt, segt], axis=0), BF16),
            jnp.asarray(inv_n))


def _mla_qkv(cq, ckv, kr, cos, sin, gqa, gkva, wuq, wuk, wuv, gq, gk, gkr, tm=256):
    s = cq.shape[0]
    h = MLA_HEADS
    row = lambda w: pl.BlockSpec((tm, w), lambda i: (i, 0))
    qind = _segment_indicators(h, [(0, MLA_NOPE, MLA_NOPE),
                                   (MLA_NOPE, MLA_ROPE, MLA_ROPE + MLA_ROPE // 2)])
    kind = _segment_indicators(h, [(0, MLA_NOPE, MLA_NOPE)])
    return pl.pallas_call(
        functools.partial(_mla_qkv_kernel, scale=MLA_QK ** -0.5 * LOG2E),
        grid=(s // tm,),
        in_specs=[row(cq.shape[1]), row(ckv.shape[1]), row(LANES), row(LANES), row(LANES),
                  _const_spec(gqa.shape), _const_spec(gkva.shape), _const_spec(wuq.shape),
                  _const_spec(wuk.shape), _const_spec(wuv.shape), _const_spec(gq.shape),
                  _const_spec(gk.shape), _const_spec(gkr.shape)]
                 + [_const_spec(a.shape) for a in qind + kind],
        out_specs=[pl.BlockSpec((h, tm, LANES), lambda i: (0, i, 0)),
                   pl.BlockSpec((h, LANES, tm), lambda i: (0, 0, i)),
                   pl.BlockSpec((h // 2, tm, 2 * LANES), lambda i: (0, i, 0))],
        out_shape=[jax.ShapeDtypeStruct((h, s, LANES), BF16),
                   jax.ShapeDtypeStruct((h, LANES, s), BF16),
                   jax.ShapeDtypeStruct((h // 2, s, 2 * LANES), BF16)],
        compiler_params=_cparams(("parallel",)),
        name="mla_qkv",
    )(cq, ckv, kr, cos, sin, gqa, gkva, wuq, wuk, wuv, gq, gk, gkr, *qind, *kind)


def _attn_rows(q_ref, kt_ref, v_ref, m_sc, acc_sc, h, r, rb, diag):
    tk = kt_ref.shape[2]
    ncol = r + rb if diag else tk
    s = jnp.dot(q_ref[h, r:r + rb, :], kt_ref[h, :, 0:ncol], preferred_element_type=F32)
    tiles = [s[:, j * LANES:(j + 1) * LANES] for j in range(ncol // LANES)]
    if diag:
        rowi = lax.broadcasted_iota(jnp.int32, (rb, LANES), 0)
        coli = lax.broadcasted_iota(jnp.int32, (rb, LANES), 1)
        for jj in range(rb // LANES):
            j = r // LANES + jj
            tiles[j] = jnp.where(coli + jj * LANES <= rowi, tiles[j], NEG)
    mloc = functools.reduce(jnp.maximum, tiles)
    m_prev = m_sc[h, r:r + rb, :]
    m_new = jnp.maximum(m_prev, jnp.max(mloc, axis=-1, keepdims=True))
    alpha = jnp.exp2(m_prev - m_new)
    p = jnp.concatenate([jnp.exp2(t - m_new).astype(BF16) for t in tiles], axis=1)
    pv = jnp.dot(p, v_ref[h // 2, 0:ncol, :], preferred_element_type=F32)
    acc_sc[h, r:r + rb, :] = jnp.concatenate([alpha, alpha], axis=1) * acc_sc[h, r:r + rb, :] + pv
    m_sc[h, r:r + rb, :] = m_new


def _mla_attn_kernel(qi_ref, ki_ref, q_ref, kt_ref, v_ref, o_ref, m_sc, acc_sc, *, rb):
    t = pl.program_id(1)
    qi = qi_ref[t]
    ki = ki_ref[t]
    tq = q_ref.shape[1]

    @pl.when(ki == 0)
    def _():
        m_sc[...] = jnp.full(m_sc.shape, NEG, F32)
        acc_sc[...] = jnp.zeros(acc_sc.shape, F32)

    hp = q_ref.shape[0]

    @pl.when(ki < qi)
    def _():
        for h in range(hp):
            for r in range(0, tq, rb):
                _attn_rows(q_ref, kt_ref, v_ref, m_sc, acc_sc, h, r, rb, diag=False)

    @pl.when(ki == qi)
    def _():
        for h in range(hp):
            for r in range(0, tq, rb):
                _attn_rows(q_ref, kt_ref, v_ref, m_sc, acc_sc, h, r, rb, diag=True)
        lane = lax.broadcasted_iota(jnp.int32, (tq, LANES), 1)
        for j in range(hp // 2):
            o0 = acc_sc[2 * j, :, 0:LANES] / acc_sc[2 * j, :, LANES:2 * LANES]
            o1 = acc_sc[2 * j + 1, :, 0:LANES] / acc_sc[2 * j + 1, :, LANES:2 * LANES]
            o_ref[:, j * LANES:(j + 1) * LANES] = jnp.where(lane < MLA_V, o0, o1).astype(o_ref.dtype)


def _mla_attn(q, kt, v, tq=1024, rb=512, hp=8):
    h, s, _ = q.shape
    nq = s // tq
    qi_tab = np.concatenate([np.full(i + 1, i, np.int32) for i in range(nq)])
    ki_tab = np.concatenate([np.arange(i + 1, dtype=np.int32) for i in range(nq)])
    grid_spec = pltpu.PrefetchScalarGridSpec(
        num_scalar_prefetch=2,
        grid=(h // hp, len(qi_tab)),
        in_specs=[pl.BlockSpec((hp, tq, LANES), lambda p, t, qi, ki: (p, qi[t], 0)),
                  pl.BlockSpec((hp, LANES, tq), lambda p, t, qi, ki: (p, 0, ki[t])),
                  pl.BlockSpec((hp // 2, tq, 2 * LANES), lambda p, t, qi, ki: (p, ki[t], 0))],
        out_specs=pl.BlockSpec((tq, hp // 2 * LANES), lambda p, t, qi, ki: (qi[t], p)),
        scratch_shapes=[pltpu.VMEM((hp, tq, LANES), F32),
                        pltpu.VMEM((hp, tq, 2 * LANES), F32)],
    )
    return pl.pallas_call(
        functools.partial(_mla_attn_kernel, rb=rb),
        grid_spec=grid_spec,
        out_shape=jax.ShapeDtypeStruct((s, h * MLA_V), BF16),
        compiler_params=_cparams(("parallel", "arbitrary")),
        name="mla_attn",
    )(jnp.asarray(qi_tab), jnp.asarray(ki_tab), q, kt, v)


def _rglru_kernel(x_ref, gate_ref, cw_ref, cb_ref, wa_ref, ba_ref, wx_ref, bx_ref, lam_ref,
                  o_ref, xext, a_sc, b_sc, h_sc, hcar):
    i = pl.program_id(0)
    tm, width = x_ref.shape
    pad = 8

    @pl.when(i == 0)
    def _():
        xext[0:pad, :] = jnp.zeros((pad, width), F32)
        hcar[...] = jnp.zeros(hcar.shape, F32)

    @pl.when(i > 0)
    def _():
        xext[0:pad, :] = xext[tm:tm + pad, :]

    xext[pad:pad + tm, :] = x_ref[...].astype(F32)
    xc = cb_ref[...]
    for kk in range(CONV_WIDTH):
        xc = xc + cw_ref[kk:kk + 1, :] * xext[pl.ds(pad - (CONV_WIDTH - 1) + kk, tm), :]
    xcb = xc.astype(BF16)

    z = -lam_ref[...]
    softplus = jnp.maximum(z, 0.0) + jnp.log(1.0 + jnp.exp(-jnp.abs(z)))
    log2_a_rate = softplus * (-LRU_C * LOG2E)
    bw = width // LRU_BLOCKS
    for b in range(LRU_BLOCKS):
        sl = slice(b * bw, (b + 1) * bw)
        xb = xcb[:, sl]
        r = _sigmoid(jnp.dot(xb, wa_ref[b], preferred_element_type=F32) + ba_ref[:, sl])
        ig = _sigmoid(jnp.dot(xb, wx_ref[b], preferred_element_type=F32) + bx_ref[:, sl])
        a = jnp.exp2(r * log2_a_rate[:, sl])
        a_sc[:, sl] = a
        b_sc[:, sl] = jnp.sqrt(1.0 - a * a) * (ig * xc[:, sl])

    def step(t, h):
        h = a_sc[pl.ds(t, 1), :] * h + b_sc[pl.ds(t, 1), :]
        h_sc[pl.ds(t, 1), :] = h
        return h

    h_last = lax.fori_loop(0, tm, step, hcar[0:1, :], unroll=8)
    hcar[0:1, :] = h_last

    g = gate_ref[...].astype(F32)
    gelu = 0.5 * g * (1.0 + jnp.tanh(math.sqrt(2.0 / math.pi) * (g + 0.044715 * (g * g * g))))
    o_ref[...] = (h_sc[...] * gelu).astype(o_ref.dtype)


def _rglru(lx, lg, cw, cb, wa, ba, wx, bx, lam, tm=512):
    s, width = lx.shape
    row = pl.BlockSpec((tm, width), lambda i: (i, 0))
    return pl.pallas_call(
        _rglru_kernel,
        grid=(s // tm,),
        in_specs=[row, row, _const_spec(cw.shape), _const_spec(cb.shape), _const_spec(wa.shape),
                  _const_spec(ba.shape), _const_spec(wx.shape), _const_spec(bx.shape),
                  _const_spec(lam.shape)],
        out_specs=row,
        out_shape=jax.ShapeDtypeStruct((s, width), BF16),
        scratch_shapes=[pltpu.VMEM((tm + 8, width), F32), pltpu.VMEM((tm, width), F32),
                        pltpu.VMEM((tm, width), F32), pltpu.VMEM((tm, width), F32),
                        pltpu.VMEM((8, width), F32)],
        compiler_params=_cparams(("arbitrary",)),
        name="rglru",
    )(lx, lg, cw, cb, wa, ba, wx, bx, lam)


def _swa_kernel(sinks_ref, q_ref, kc_ref, kp_ref, vc_ref, vp_ref, gq_ref, gk_ref, o_ref, *, scale):
    n = pl.program_id(0)
    w = q_ref.shape[0]
    qi = lax.broadcasted_iota(jnp.int32, (w, 2 * w), 0)
    kj = lax.broadcasted_iota(jnp.int32, (w, 2 * w), 1)
    delta = qi + w - kj
    valid = (delta >= 0) & (delta < w) & ((n > 0) | (kj >= w))
    lane = lax.broadcasted_iota(jnp.int32, (w, LANES), 1)
    gq = gq_ref[...] * scale
    gk = gk_ref[...]
    ones = jnp.ones((2 * w, LANES), BF16)
    for kv in range(SWA_KV_HEADS):
        sl = slice(kv * LANES, (kv + 1) * LANES)
        kk = jnp.concatenate([kp_ref[:, sl], kc_ref[:, sl]], axis=0).astype(F32)
        k_ms = jnp.sum(kk * kk, axis=-1, keepdims=True) * (1.0 / SWA_DIM)
        kk = (kk * lax.rsqrt(k_ms + EPS) * gk).astype(BF16)
        vv = jnp.concatenate(
            [jnp.concatenate([vp_ref[:, sl], vc_ref[:, sl]], axis=0), ones], axis=1)
        outs = []
        for g in range(SWA_GROUP):
            head = kv * SWA_GROUP + g
            q = q_ref[:, head * LANES:(head + 1) * LANES].astype(F32)
            q_ms = jnp.sum(q * q, axis=-1, keepdims=True) * (1.0 / SWA_DIM)
            q = (q * lax.rsqrt(q_ms + EPS) * gq).astype(BF16)
            s = lax.dot_general(q, kk, (((1,), (1,)), ((), ())), preferred_element_type=F32)
            s = jnp.where(valid, s, NEG)
            sink = sinks_ref[head]
            m = jnp.maximum(jnp.max(s, axis=-1, keepdims=True), sink)
            pv = jnp.dot(jnp.exp2(s - m).astype(BF16), vv, preferred_element_type=F32)
            outs.append(pv[:, 0:LANES] / (pv[:, LANES:2 * LANES] + jnp.exp2(sink - m)))
        for j in range(SWA_GROUP // 2):
            pair = kv * (SWA_GROUP // 2) + j
            o_ref[:, pair * LANES:(pair + 1) * LANES] = jnp.where(
                lane < SWA_DIM, outs[2 * j], outs[2 * j + 1]).astype(o_ref.dtype)


def _swa(sinks, sq, sk, sv, gq, gk):
    s = sq.shape[0]
    w = WINDOW
    cur = lambda width: pl.BlockSpec((w, width), lambda n: (n, 0))
    prev = lambda width: pl.BlockSpec((w, width), lambda n: (jnp.maximum(n - 1, 0), 0))
    return pl.pallas_call(
        functools.partial(_swa_kernel, scale=SWA_DIM ** -0.5 * LOG2E),
        grid=(s // w,),
        in_specs=[pl.BlockSpec(memory_space=pltpu.SMEM),
                  cur(sq.shape[1]), cur(sk.shape[1]), prev(sk.shape[1]),
                  cur(sv.shape[1]), prev(sv.shape[1]),
                  _const_spec(gq.shape), _const_spec(gk.shape)],
        out_specs=cur(SWA_HEADS * SWA_DIM),
        out_shape=jax.ShapeDtypeStruct((s, SWA_HEADS * SWA_DIM), BF16),
        compiler_params=_cparams(("parallel",)),
        name="swa",
    )(sinks, sq, sk, sk, sv, sv, gq, gk)


def _merge_kernel(x_ref, oa_ref, ob_ref, oc_ref, g_ref, wb_ref, wo_ref, y_ref):
    d = x_ref.shape[1]
    merged = None
    for b, o_ref in enumerate((oa_ref, ob_ref, oc_ref)):
        br = jnp.dot(o_ref[...], wb_ref[b], preferred_element_type=F32)
        term = _sigmoid(g_ref[:, b * d:(b + 1) * d].astype(F32)) * br
        merged = term if merged is None else merged + term
    y_ref[...] = x_ref[...] + jnp.dot(merged.astype(BF16), wo_ref[...],
                                      preferred_element_type=F32)


def _merge(x2d, oa, ob, oc, gates, wb, wo, tm=1024):
    s, d = x2d.shape
    row = lambda w: pl.BlockSpec((tm, w), lambda i: (i, 0))
    return pl.pallas_call(
        _merge_kernel,
        grid=(s // tm,),
        in_specs=[row(d), row(d), row(d), row(d), row(N_BRANCH * d),
                  _const_spec(wb.shape), _const_spec(wo.shape)],
        out_specs=row(d),
        out_shape=jax.ShapeDtypeStruct((s, d), F32),
        compiler_params=_cparams(("parallel",)),
        name="merge",
    )(x2d, oa, ob, oc, gates, wb, wo)


def _ffn_kernel(x_ref, g_ref, w1_ref, w2_ref, y_ref, *, chunk):
    x = x_ref[...]
    hn = _rms(x, g_ref[...]).astype(BF16)
    acc = x
    for c in range(0, w1_ref.shape[1], chunk):
        a = jnp.maximum(jnp.dot(hn, w1_ref[:, c:c + chunk], preferred_element_type=F32), 0.0)
        acc = acc + jnp.dot((a * a).astype(BF16), w2_ref[c:c + chunk, :],
                            preferred_element_type=F32)
    y_ref[...] = acc


def _ffn(x2d, gain, w1, w2, tm=512):
    s, d = x2d.shape
    row = pl.BlockSpec((tm, d), lambda i: (i, 0))
    return pl.pallas_call(
        functools.partial(_ffn_kernel, chunk=1024),
        grid=(s // tm,),
        in_specs=[row, _const_spec(gain.shape), _const_spec(w1.shape), _const_spec(w2.shape)],
        out_specs=row,
        out_shape=jax.ShapeDtypeStruct((s, d), F32),
        compiler_params=_cparams(("parallel",)),
        name="ffn",
    )(x2d, gain, w1, w2)


def _pad_last(a, width):
    return jnp.pad(a, [(0, 0)] * (a.ndim - 1) + [(0, width - a.shape[-1])])


def _rope_tables(seq):
    half = MLA_ROPE // 2
    pos = jnp.arange(seq, dtype=F32)
    inv = ROPE_THETA ** (-jnp.arange(0, MLA_ROPE, 2, dtype=F32) / MLA_ROPE)
    ang = pos[:, None] * inv[None, :]
    cos, sin = lax.optimization_barrier((jnp.cos(ang), jnp.sin(ang)))
    assert cos.shape[1] == half
    ones = jnp.ones((seq, MLA_NOPE), F32)
    tail = jnp.zeros((seq, LANES - MLA_QK), F32)
    nope0 = jnp.zeros((seq, MLA_NOPE), F32)
    cos_t = jnp.concatenate([ones, cos, cos, tail], axis=1)
    sin_t = jnp.concatenate([nope0, -sin, sin, tail], axis=1)
    return cos_t, sin_t


def _rope_slot(a):
    half = MLA_ROPE // 2
    return _pad_last(jnp.concatenate([a, a[..., :half]], axis=-1), LANES - MLA_NOPE)


def _layer_weights(l, d, w_in, mla_q_a_norm, mla_kv_a_norm, w_uq, w_ukv, mla_q_norm, mla_k_norm,
                   swa_q_norm, swa_k_norm):
    q_rank = w_uq.shape[1]
    kv_rank = w_ukv.shape[1]
    lru_w = d
    sizes = (q_rank, kv_rank, MLA_ROPE, lru_w, lru_w, SWA_HEADS * SWA_DIM,
             SWA_KV_HEADS * SWA_DIM, SWA_KV_HEADS * SWA_DIM, N_BRANCH * d)
    assert sum(sizes) == w_in.shape[2]
    offs = np.cumsum((0,) + sizes)
    seg = [w_in[l][:, offs[i]:offs[i + 1]] for i in range(len(sizes))]
    w_cq, w_ckv, w_kr, w_lx, w_lg, w_sq, w_sk, w_sv, w_g = seg
    kr_slot = jnp.concatenate([jnp.zeros((d, MLA_NOPE), F32), _rope_slot(w_kr)], axis=1)
    sq_pad = _pad_last(w_sq.reshape(d, SWA_HEADS, SWA_DIM), LANES).reshape(d, SWA_HEADS * LANES)
    sk_pad = _pad_last(w_sk.reshape(d, SWA_KV_HEADS, SWA_DIM), LANES).reshape(
        d, SWA_KV_HEADS * LANES)
    sv3 = w_sv.reshape(d, SWA_KV_HEADS, SWA_DIM)
    sv_dup = jnp.concatenate([sv3, sv3], axis=-1).reshape(d, SWA_KV_HEADS * LANES)
    w_cat = jnp.concatenate([w_cq, w_ckv, kr_slot, w_lx, w_lg, sq_pad, sk_pad, sv_dup, w_g],
                            axis=1).astype(BF16)
    seg_widths = (q_rank, kv_rank, LANES, lru_w, lru_w, SWA_HEADS * LANES, SWA_KV_HEADS * LANES,
                  SWA_KV_HEADS * LANES, N_BRANCH * d)

    wuq3 = w_uq[l].reshape(q_rank, MLA_HEADS, MLA_QK)
    wuq = jnp.concatenate([wuq3[..., :MLA_NOPE], _rope_slot(wuq3[..., MLA_NOPE:])],
                          axis=-1).reshape(q_rank, MLA_HEADS * LANES).astype(BF16)
    wukv = w_ukv[l].reshape(kv_rank, MLA_HEADS, MLA_NOPE + MLA_V)
    wuk = _pad_last(wukv[..., :MLA_NOPE], LANES).reshape(kv_rank, MLA_HEADS * LANES).astype(BF16)
    wuv = wukv[..., MLA_NOPE:].reshape(kv_rank, MLA_HEADS * MLA_V).astype(BF16)
    gq = jnp.concatenate([mla_q_norm[l][:MLA_NOPE], _rope_slot(mla_q_norm[l][MLA_NOPE:])])[None]
    gk = _pad_last(mla_k_norm[l][None, :MLA_NOPE], LANES)
    gkr = jnp.concatenate([jnp.zeros((MLA_NOPE,), F32), _rope_slot(mla_k_norm[l][MLA_NOPE:])])[None]
    sgq = _pad_last(swa_q_norm[l][None, :], LANES)
    sgk = _pad_last(swa_k_norm[l][None, :], LANES)
    return dict(w_cat=w_cat, seg_widths=seg_widths, wuq=wuq, wuk=wuk, wuv=wuv, gq=gq, gk=gk,
                gkr=gkr, gqa=mla_q_a_norm[l][None, :], gkva=mla_kv_a_norm[l][None, :],
                sgq=sgq, sgk=sgk)


def kernel(x, norm1, w_in, mla_q_a_norm, mla_kv_a_norm, w_uq, w_ukv, mla_q_norm, mla_k_norm,
           conv_w, conv_b, w_rg_a, b_rg_a, w_rg_x, b_rg_x, lru_lambda, swa_q_norm, swa_k_norm,
           swa_sinks, w_branch, w_out, norm2, w_ff1, w_ff2):
    batch, seq, d = x.shape
    assert batch == 1
    depth = w_in.shape[0]
    cos_t, sin_t = _rope_tables(seq)
    h = x.reshape(seq, d)
    for l in range(depth):
        lw = _layer_weights(l, d, w_in, mla_q_a_norm, mla_kv_a_norm, w_uq, w_ukv, mla_q_norm,
                            mla_k_norm, swa_q_norm, swa_k_norm)
        cq, ckv, kr, lx, lg, sq, sk, sv, gates = _in_proj(
            h, norm1[l][None, :], lw["w_cat"], lw["seg_widths"])
        q, kt, v = _mla_qkv(cq, ckv, kr, cos_t, sin_t, lw["gqa"], lw["gkva"], lw["wuq"],
                            lw["wuk"], lw["wuv"], lw["gq"], lw["gk"], lw["gkr"])
        o_a = _mla_attn(q, kt, v)
        o_b = _rglru(lx, lg, conv_w[l], conv_b[l][None, :], w_rg_a[l].astype(BF16),
                     b_rg_a[l][None, :], w_rg_x[l].astype(BF16), b_rg_x[l][None, :],
                     lru_lambda[l][None, :])
        o_c = _swa(swa_sinks[l] * LOG2E, sq, sk, sv, lw["sgq"], lw["sgk"])
        h = _merge(h, o_a, o_b, o_c, gates, w_branch[l].astype(BF16), w_out[l].astype(BF16))
        h = _ffn(h, norm2[l][None, :], w_ff1[l].astype(BF16), w_ff2[l].astype(BF16))
    return h.reshape(batch, seq, d)
```

```python
import functools
import math

import jax
import jax.numpy as jnp
import numpy as np
from jax import lax
from jax.experimental import pallas as pl
from jax.experimental.pallas import tpu as pltpu

F32 = jnp.float32
BF16 = jnp.bfloat16

LANES = 128

MLA_HEADS = 16
MLA_NOPE = 64
MLA_ROPE = 32
MLA_QK = MLA_NOPE + MLA_ROPE
MLA_V = 64
ROPE_THETA = 10000.0
LRU_BLOCKS = 4
CONV_WIDTH = 4
LRU_C = 8.0
SWA_HEADS = 16
SWA_KV_HEADS = 2
SWA_GROUP = SWA_HEADS // SWA_KV_HEADS
SWA_DIM = 64
WINDOW = 128
N_BRANCH = 3
EPS = 1e-6
NEG = -1e30
LOG2E = math.log2(math.e)

VMEM_LIMIT = 56 * 1024 * 1024


def _cparams(sem):
    return pltpu.CompilerParams(dimension_semantics=sem, vmem_limit_bytes=VMEM_LIMIT)


def _const_spec(shape):
    nd = len(shape)
    return pl.BlockSpec(shape, lambda *_: (0,) * nd, pipeline_mode=pl.Buffered(1))


def _rms(xf, gain):
    ms = jnp.mean(xf * xf, axis=-1, keepdims=True)
    return xf * lax.rsqrt(ms + EPS) * gain


def _sigmoid(x):
    return 1.0 / (1.0 + jnp.exp2(x * (-LOG2E)))


def _in_proj_kernel(x_ref, g_ref, w_ref, *out_refs, seg_widths, chunk):
    xn = _rms(x_ref[...], g_ref[...]).astype(BF16)
    lo = 0
    for o_ref, width in zip(out_refs, seg_widths):
        for c in range(0, width, chunk):
            ce = min(c + chunk, width)
            o_ref[:, c:ce] = jnp.dot(
                xn, w_ref[:, lo + c:lo + ce], preferred_element_type=F32).astype(o_ref.dtype)
        lo += width


def _in_proj(x2d, gain, w_cat, seg_widths, tm=512):
    s, d = x2d.shape
    n = w_cat.shape[1]
    assert sum(seg_widths) == n and s % tm == 0
    return pl.pallas_call(
        functools.partial(_in_proj_kernel, seg_widths=tuple(seg_widths), chunk=1024),
        grid=(s // tm,),
        in_specs=[pl.BlockSpec((tm, d), lambda i: (i, 0)),
                  _const_spec((1, d)),
                  _const_spec((d, n))],
        out_specs=[pl.BlockSpec((tm, w), lambda i: (i, 0)) for w in seg_widths],
        out_shape=[jax.ShapeDtypeStruct((s, w), BF16) for w in seg_widths],
        compiler_params=_cparams(("parallel",)),
        name="in_proj",
    )(x2d, gain, w_cat)


def _segment_rsqrt(y, seg_ref, segt_ref, inv_n_ref):
    ss = jnp.dot((y * y).astype(BF16), seg_ref[...], preferred_element_type=F32)
    inv = lax.rsqrt(ss * inv_n_ref[...] + EPS)
    hi = inv.astype(BF16)
    lo = (inv - hi.astype(F32)).astype(BF16)
    return jnp.dot(jnp.concatenate([hi, lo], axis=1), segt_ref[...], preferred_element_type=F32)


def _mla_qkv_kernel(cq_ref, ckv_ref, kr_ref, cos_ref, sin_ref, gqa_ref, gkva_ref,
                    wuq_ref, wuk_ref, wuv_ref, gq_ref, gk_ref, gkr_ref,
                    qseg_ref, qsegt_ref, qn_ref, kseg_ref, ksegt_ref, kn_ref,
                    q_out, kt_out, v_out, *, scale):
    tm = cq_ref.shape[0]
    lane = lax.broadcasted_iota(jnp.int32, (tm, LANES), 1)
    is_rope = (lane >= MLA_NOPE) & (lane < MLA_QK)
    cos = cos_ref[...]
    sin = sin_ref[...]

    def rope(y):
        return y * cos + pltpu.roll(y, LANES - MLA_ROPE // 2, 1) * sin

    cqn = _rms(cq_ref[...].astype(F32), gqa_ref[...]).astype(BF16)
    kvn = _rms(ckv_ref[...].astype(F32), gkva_ref[...]).astype(BF16)

    kr = kr_ref[...].astype(F32)
    kr_ms = jnp.sum(jnp.where(is_rope, kr * kr, 0.0), axis=-1, keepdims=True) * (1.0 / MLA_ROPE)
    kpe = rope(kr * lax.rsqrt(kr_ms + EPS) * gkr_ref[...])

    gq = gq_ref[...] * scale
    gk = gk_ref[...]
    yq = jnp.dot(cqn, wuq_ref[...], preferred_element_type=F32)
    yq = yq * _segment_rsqrt(yq, qseg_ref, qsegt_ref, qn_ref)
    yk = jnp.dot(kvn, wuk_ref[...], preferred_element_type=F32)
    yk = yk * _segment_rsqrt(yk, kseg_ref, ksegt_ref, kn_ref)
    for h in range(MLA_HEADS):
        sl = slice(h * LANES, (h + 1) * LANES)
        q_out[h] = rope(yq[:, sl] * gq).astype(BF16)
        kt_out[h] = (yk[:, sl] * gk + kpe).T.astype(BF16)

    ones = jnp.ones((tm, LANES), BF16)
    for g in range(MLA_HEADS // 4):
        v2 = jnp.dot(kvn, wuv_ref[:, 2 * g * LANES:(2 * g + 2) * LANES],
                     preferred_element_type=F32).astype(BF16)
        for i in range(2):
            v_out[2 * g + i, :, 0:LANES] = v2[:, i * LANES:(i + 1) * LANES]
            v_out[2 * g + i, :, LANES:2 * LANES] = ones


def _segment_indicators(n_slots, segments):
    assert n_slots * len(segments) <= LANES
    seg = np.zeros((n_slots * LANES, LANES), np.float32)
    segt = np.zeros((LANES, n_slots * LANES), np.float32)
    inv_n = np.zeros((1, LANES), np.float32)
    col = 0
    for slot in range(n_slots):
        for start, n_sum, n_spread in segments:
            lo = slot * LANES + start
            seg[lo:lo + n_sum, col] = 1.0
            segt[col, lo:lo + n_spread] = 1.0
            inv_n[0, col] = 1.0 / n_sum
            col += 1
    return (jnp.asarray(seg, BF16), jnp.asarray(np.concatenate([segt, segt], axis=0), BF16),
            jnp.asarray(inv_n))


def _mla_qkv(cq, ckv, kr, cos, sin, gqa, gkva, wuq, wuk, wuv, gq, gk, gkr, tm=256):
    s = cq.shape[0]
    h = MLA_HEADS
    row = lambda w: pl.BlockSpec((tm, w), lambda i: (i, 0))
    qind = _segment_indicators(h, [(0, MLA_NOPE, MLA_NOPE),
                                   (MLA_NOPE, MLA_ROPE, MLA_ROPE + MLA_ROPE // 2)])
    kind = _segment_indicators(h, [(0, MLA_NOPE, MLA_NOPE)])
    return pl.pallas_call(
        functools.partial(_mla_qkv_kernel, scale=MLA_QK ** -0.5 * LOG2E),
        grid=(s // tm,),
        in_specs=[row(cq.shape[1]), row(ckv.shape[1]), row(LANES), row(LANES), row(LANES),
                  _const_spec(gqa.shape), _const_spec(gkva.shape), _const_spec(wuq.shape),
                  _const_spec(wuk.shape), _const_spec(wuv.shape), _const_spec(gq.shape),
                  _const_spec(gk.shape), _const_spec(gkr.shape)]
                 + [_const_spec(a.shape) for a in qind + kind],
        out_specs=[pl.BlockSpec((h, tm, LANES), lambda i: (0, i, 0)),
                   pl.BlockSpec((h, LANES, tm), lambda i: (0, 0, i)),
                   pl.BlockSpec((h // 2, tm, 2 * LANES), lambda i: (0, i, 0))],
        out_shape=[jax.ShapeDtypeStruct((h, s, LANES), BF16),
                   jax.ShapeDtypeStruct((h, LANES, s), BF16),
                   jax.ShapeDtypeStruct((h // 2, s, 2 * LANES), BF16)],
        compiler_params=_cparams(("parallel",)),
        name="mla_qkv",
    )(cq, ckv, kr, cos, sin, gqa, gkva, wuq, wuk, wuv, gq, gk, gkr, *qind, *kind)


def _attn_rows(q_ref, kt_ref, v_ref, m_sc, acc_sc, h, r, rb, diag):
    tk = kt_ref.shape[2]
    ncol = r + rb if diag else tk
    s = jnp.dot(q_ref[h, r:r + rb, :], kt_ref[h, :, 0:ncol], preferred_element_type=F32)
    tiles = [s[:, j * LANES:(j + 1) * LANES] for j in range(ncol // LANES)]
    if diag:
        rowi = lax.broadcasted_iota(jnp.int32, (rb, LANES), 0)
        coli = lax.broadcasted_iota(jnp.int32, (rb, LANES), 1)
        for jj in range(rb // LANES):
            j = r // LANES + jj
            tiles[j] = jnp.where(coli + jj * LANES <= rowi, tiles[j], NEG)
    mloc = functools.reduce(jnp.maximum, tiles)
    m_prev = m_sc[h, r:r + rb, :]
    m_new = jnp.maximum(m_prev, jnp.max(mloc, axis=-1, keepdims=True))
    alpha = jnp.exp2(m_prev - m_new)
    p = jnp.concatenate([jnp.exp2(t - m_new).astype(BF16) for t in tiles], axis=1)
    pv = jnp.dot(p, v_ref[h // 2, 0:ncol, :], preferred_element_type=F32)
    acc_sc[h, r:r + rb, :] = jnp.concatenate([alpha, alpha], axis=1) * acc_sc[h, r:r + rb, :] + pv
    m_sc[h, r:r + rb, :] = m_new


def _mla_attn_kernel(qi_ref, ki_ref, q_ref, kt_ref, v_ref, o_ref, m_sc, acc_sc, *, rb):
    t = pl.program_id(1)
    qi = qi_ref[t]
    ki = ki_ref[t]
    tq = q_ref.shape[1]

    @pl.when(ki == 0)
    def _():
        m_sc[...] = jnp.full(m_sc.shape, NEG, F32)
        acc_sc[...] = jnp.zeros(acc_sc.shape, F32)

    hp = q_ref.shape[0]

    @pl.when(ki < qi)
    def _():
        for h in range(hp):
            for r in range(0, tq, rb):
                _attn_rows(q_ref, kt_ref, v_ref, m_sc, acc_sc, h, r, rb, diag=False)

    @pl.when(ki == qi)
    def _():
        for h in range(hp):
            for r in range(0, tq, rb):
                _attn_rows(q_ref, kt_ref, v_ref, m_sc, acc_sc, h, r, rb, diag=True)
        lane = lax.broadcasted_iota(jnp.int32, (tq, LANES), 1)
        for j in range(hp // 2):
            o0 = acc_sc[2 * j, :, 0:LANES] / acc_sc[2 * j, :, LANES:2 * LANES]
            o1 = acc_sc[2 * j + 1, :, 0:LANES] / acc_sc[2 * j + 1, :, LANES:2 * LANES]
            o_ref[:, j * LANES:(j + 1) * LANES] = jnp.where(lane < MLA_V, o0, o1).astype(o_ref.dtype)


def _mla_attn(q, kt, v, tq=1024, rb=512, hp=8):
    h, s, _ = q.shape
    nq = s // tq
    qi_tab = np.concatenate([np.full(i + 1, i, np.int32) for i in range(nq)])
    ki_tab = np.concatenate([np.arange(i + 1, dtype=np.int32) for i in range(nq)])
    grid_spec = pltpu.PrefetchScalarGridSpec(
        num_scalar_prefetch=2,
        grid=(h // hp, len(qi_tab)),
        in_specs=[pl.BlockSpec((hp, tq, LANES), lambda p, t, qi, ki: (p, qi[t], 0)),
                  pl.BlockSpec((hp, LANES, tq), lambda p, t, qi, ki: (p, 0, ki[t])),
                  pl.BlockSpec((hp // 2, tq, 2 * LANES), lambda p, t, qi, ki: (p, ki[t], 0))],
        out_specs=pl.BlockSpec((tq, hp // 2 * LANES), lambda p, t, qi, ki: (qi[t], p)),
        scratch_shapes=[pltpu.VMEM((hp, tq, LANES), F32),
                        pltpu.VMEM((hp, tq, 2 * LANES), F32)],
    )
    return pl.pallas_call(
        functools.partial(_mla_attn_kernel, rb=rb),
        grid_spec=grid_spec,
        out_shape=jax.ShapeDtypeStruct((s, h * MLA_V), BF16),
        compiler_params=_cparams(("parallel", "arbitrary")),
        name="mla_attn",
    )(jnp.asarray(qi_tab), jnp.asarray(ki_tab), q, kt, v)


def _rglru_kernel(x_ref, gate_ref, cw_ref, cb_ref, wa_ref, ba_ref, wx_ref, bx_ref, lam_ref,
                  o_ref, xext, a_sc, b_sc, h_sc, hcar):
    i = pl.program_id(0)
    tm, width = x_ref.shape
    pad = 8

    @pl.when(i == 0)
    def _():
        xext[0:pad, :] = jnp.zeros((pad, width), F32)
        hcar[...] = jnp.zeros(hcar.shape, F32)

    @pl.when(i > 0)
    def _():
        xext[0:pad, :] = xext[tm:tm + pad, :]

    xext[pad:pad + tm, :] = x_ref[...].astype(F32)
    xc = cb_ref[...]
    for kk in range(CONV_WIDTH):
        xc = xc + cw_ref[kk:kk + 1, :] * xext[pl.ds(pad - (CONV_WIDTH - 1) + kk, tm), :]
    xcb = xc.astype(BF16)

    z = -lam_ref[...]
    softplus = jnp.maximum(z, 0.0) + jnp.log(1.0 + jnp.exp(-jnp.abs(z)))
    log2_a_rate = softplus * (-LRU_C * LOG2E)
    bw = width // LRU_BLOCKS
    for b in range(LRU_BLOCKS):
        sl = slice(b * bw, (b + 1) * bw)
        xb = xcb[:, sl]
        r = _sigmoid(jnp.dot(xb, wa_ref[b], preferred_element_type=F32) + ba_ref[:, sl])
        ig = _sigmoid(jnp.dot(xb, wx_ref[b], preferred_element_type=F32) + bx_ref[:, sl])
        a = jnp.exp2(r * log2_a_rate[:, sl])
        a_sc[:, sl] = a
        b_sc[:, sl] = jnp.sqrt(1.0 - a * a) * (ig * xc[:, sl])

    def step(t, h):
        h = a_sc[pl.ds(t, 1), :] * h + b_sc[pl.ds(t, 1), :]
        h_sc[pl.ds(t, 1), :] = h
        return h

    h_last = lax.fori_loop(0, tm, step, hcar[0:1, :], unroll=8)
    hcar[0:1, :] = h_last

    g = gate_ref[...].astype(F32)
    gelu = 0.5 * g * (1.0 + jnp.tanh(math.sqrt(2.0 / math.pi) * (g + 0.044715 * (g * g * g))))
    o_ref[...] = (h_sc[...] * gelu).astype(o_ref.dtype)


def _rglru(lx, lg, cw, cb, wa, ba, wx, bx, lam, tm=512):
    s, width = lx.shape
    row = pl.BlockSpec((tm, width), lambda i: (i, 0))
    return pl.pallas_call(
        _rglru_kernel,
        grid=(s // tm,),
        in_specs=[row, row, _const_spec(cw.shape), _const_spec(cb.shape), _const_spec(wa.shape),
                  _const_spec(ba.shape), _const_spec(wx.shape), _const_spec(bx.shape),
                  _const_spec(lam.shape)],
        out_specs=row,
        out_shape=jax.ShapeDtypeStruct((s, width), BF16),
        scratch_shapes=[pltpu.VMEM((tm + 8, width), F32), pltpu.VMEM((tm, width), F32),
                        pltpu.VMEM((tm, width), F32), pltpu.VMEM((tm, width), F32),
                        pltpu.VMEM((8, width), F32)],
        compiler_params=_cparams(("arbitrary",)),
        name="rglru",
    )(lx, lg, cw, cb, wa, ba, wx, bx, lam)


def _swa_kernel(sinks_ref, q_ref, kc_ref, kp_ref, vc_ref, vp_ref, gq_ref, gk_ref, o_ref, *, scale):
    n = pl.program_id(0)
    w = q_ref.shape[0]
    qi = lax.broadcasted_iota(jnp.int32, (w, 2 * w), 0)
    kj = lax.broadcasted_iota(jnp.int32, (w, 2 * w), 1)
    delta = qi + w - kj
    valid = (delta >= 0) & (delta < w) & ((n > 0) | (kj >= w))
    lane = lax.broadcasted_iota(jnp.int32, (w, LANES), 1)
    low = lane < SWA_DIM
    klow = lax.broadcasted_iota(jnp.int32, (2 * w, LANES), 1) < SWA_DIM
    gq = gq_ref[...] * scale
    gk = gk_ref[...]
    ones = jnp.ones((2 * w, LANES), BF16)
    for kv in range(SWA_KV_HEADS):
        sl = slice(kv * LANES, (kv + 1) * LANES)
        kk = jnp.concatenate([kp_ref[:, sl], kc_ref[:, sl]], axis=0).astype(F32)
        k_ms = jnp.sum(kk * kk, axis=-1, keepdims=True) * (0.5 / SWA_DIM)
        kk = kk * lax.rsqrt(k_ms + EPS) * gk
        kbd = jnp.concatenate([jnp.where(klow, kk, 0.0), jnp.where(klow, 0.0, kk)],
                              axis=0).astype(BF16)
        vv = jnp.concatenate(
            [jnp.concatenate([vp_ref[:, sl], vc_ref[:, sl]], axis=0), ones], axis=1)
        nslot = SWA_GROUP // 2
        slot0 = kv * nslot
        q = jnp.concatenate([q_ref[:, (slot0 + j) * LANES:(slot0 + j + 1) * LANES]
                             for j in range(nslot)], axis=0).astype(F32)
        low4 = jnp.concatenate([low] * nslot, axis=0)
        q2 = q * q
        ms_lo = jnp.sum(jnp.where(low4, q2, 0.0), axis=-1, keepdims=True) * (1.0 / SWA_DIM)
        ms_hi = jnp.sum(jnp.where(low4, 0.0, q2), axis=-1, keepdims=True) * (1.0 / SWA_DIM)
        r = jnp.where(low4, lax.rsqrt(ms_lo + EPS), lax.rsqrt(ms_hi + EPS))
        qn = (q * r * gq).astype(BF16)
        s2 = lax.dot_general(qn, kbd, (((1,), (1,)), ((), ())), preferred_element_type=F32)
        ps, sink_terms = [], []
        for j in range(nslot):
            for i in range(2):
                s = jnp.where(valid, s2[j * w:(j + 1) * w, i * 2 * w:(i + 1) * 2 * w], NEG)
                sink = sinks_ref[2 * (slot0 + j) + i]
                m = jnp.maximum(jnp.max(s, axis=-1, keepdims=True), sink)
                ps.append(jnp.exp2(s - m).astype(BF16))
                sink_terms.append(jnp.exp2(sink - m))
        pv = jnp.dot(jnp.concatenate(ps, axis=0), vv, preferred_element_type=F32)
        for j in range(nslot):
            outs = []
            for i in range(2):
                rows = slice((2 * j + i) * w, (2 * j + i + 1) * w)
                outs.append(pv[rows, 0:LANES] / (pv[rows, LANES:2 * LANES] + sink_terms[2 * j + i]))
            o_ref[:, (slot0 + j) * LANES:(slot0 + j + 1) * LANES] = jnp.where(
                low, outs[0], outs[1]).astype(o_ref.dtype)


def _swa(sinks, sq, sk, sv, gq, gk):
    s = sq.shape[0]
    w = WINDOW
    cur = lambda width: pl.BlockSpec((w, width), lambda n: (n, 0))
    prev = lambda width: pl.BlockSpec((w, width), lambda n: (jnp.maximum(n - 1, 0), 0))
    return pl.pallas_call(
        functools.partial(_swa_kernel, scale=SWA_DIM ** -0.5 * LOG2E),
        grid=(s // w,),
        in_specs=[pl.BlockSpec(memory_space=pltpu.SMEM),
                  cur(sq.shape[1]), cur(sk.shape[1]), prev(sk.shape[1]),
                  cur(sv.shape[1]), prev(sv.shape[1]),
                  _const_spec(gq.shape), _const_spec(gk.shape)],
        out_specs=cur(SWA_HEADS * SWA_DIM),
        out_shape=jax.ShapeDtypeStruct((s, SWA_HEADS * SWA_DIM), BF16),
        compiler_params=_cparams(("parallel",)),
        name="swa",
    )(sinks, sq, sk, sk, sv, sv, gq, gk)


def _merge_kernel(x_ref, oa_ref, ob_ref, oc_ref, g_ref, wb_ref, wo_ref, y_ref):
    d = x_ref.shape[1]
    merged = None
    for b, o_ref in enumerate((oa_ref, ob_ref, oc_ref)):
        br = jnp.dot(o_ref[...], wb_ref[b], preferred_element_type=F32)
        term = _sigmoid(g_ref[:, b * d:(b + 1) * d].astype(F32)) * br
        merged = term if merged is None else merged + term
    y_ref[...] = x_ref[...] + jnp.dot(merged.astype(BF16), wo_ref[...],
                                      preferred_element_type=F32)


def _merge(x2d, oa, ob, oc, gates, wb, wo, tm=1024):
    s, d = x2d.shape
    row = lambda w: pl.BlockSpec((tm, w), lambda i: (i, 0))
    return pl.pallas_call(
        _merge_kernel,
        grid=(s // tm,),
        in_specs=[row(d), row(d), row(d), row(d), row(N_BRANCH * d),
                  _const_spec(wb.shape), _const_spec(wo.shape)],
        out_specs=row(d),
        out_shape=jax.ShapeDtypeStruct((s, d), F32),
        compiler_params=_cparams(("parallel",)),
        name="merge",
    )(x2d, oa, ob, oc, gates, wb, wo)


def _ffn_kernel(x_ref, g_ref, w1_ref, w2_ref, y_ref, *, chunk):
    x = x_ref[...]
    hn = _rms(x, g_ref[...]).astype(BF16)
    acc = x
    for c in range(0, w1_ref.shape[1], chunk):
        a = jnp.maximum(jnp.dot(hn, w1_ref[:, c:c + chunk], preferred_element_type=F32), 0.0)
        acc = acc + jnp.dot((a * a).astype(BF16), w2_ref[c:c + chunk, :],
                            preferred_element_type=F32)
    y_ref[...] = acc


def _ffn(x2d, gain, w1, w2, tm=512):
    s, d = x2d.shape
    row = pl.BlockSpec((tm, d), lambda i: (i, 0))
    return pl.pallas_call(
        functools.partial(_ffn_kernel, chunk=1024),
        grid=(s // tm,),
        in_specs=[row, _const_spec(gain.shape), _const_spec(w1.shape), _const_spec(w2.shape)],
        out_specs=row,
        out_shape=jax.ShapeDtypeStruct((s, d), F32),
        compiler_params=_cparams(("parallel",)),
        name="ffn",
    )(x2d, gain, w1, w2)


def _pad_last(a, width):
    return jnp.pad(a, [(0, 0)] * (a.ndim - 1) + [(0, width - a.shape[-1])])


def _rope_tables(seq):
    half = MLA_ROPE // 2
    pos = jnp.arange(seq, dtype=F32)
    inv = ROPE_THETA ** (-jnp.arange(0, MLA_ROPE, 2, dtype=F32) / MLA_ROPE)
    ang = pos[:, None] * inv[None, :]
    cos, sin = lax.optimization_barrier((jnp.cos(ang), jnp.sin(ang)))
    assert cos.shape[1] == half
    ones = jnp.ones((seq, MLA_NOPE), F32)
    tail = jnp.zeros((seq, LANES - MLA_QK), F32)
    nope0 = jnp.zeros((seq, MLA_NOPE), F32)
    cos_t = jnp.concatenate([ones, cos, cos, tail], axis=1)
    sin_t = jnp.concatenate([nope0, -sin, sin, tail], axis=1)
    return cos_t, sin_t


def _rope_slot(a):
    half = MLA_ROPE // 2
    return _pad_last(jnp.concatenate([a, a[..., :half]], axis=-1), LANES - MLA_NOPE)


def _layer_weights(l, d, w_in, mla_q_a_norm, mla_kv_a_norm, w_uq, w_ukv, mla_q_norm, mla_k_norm,
                   swa_q_norm, swa_k_norm):
    q_rank = w_uq.shape[1]
    kv_rank = w_ukv.shape[1]
    lru_w = d
    sizes = (q_rank, kv_rank, MLA_ROPE, lru_w, lru_w, SWA_HEADS * SWA_DIM,
             SWA_KV_HEADS * SWA_DIM, SWA_KV_HEADS * SWA_DIM, N_BRANCH * d)
    assert sum(sizes) == w_in.shape[2]
    offs = np.cumsum((0,) + sizes)
    seg = [w_in[l][:, offs[i]:offs[i + 1]] for i in range(len(sizes))]
    w_cq, w_ckv, w_kr, w_lx, w_lg, w_sq, w_sk, w_sv, w_g = seg
    kr_slot = jnp.concatenate([jnp.zeros((d, MLA_NOPE), F32), _rope_slot(w_kr)], axis=1)
    sk3 = w_sk.reshape(d, SWA_KV_HEADS, SWA_DIM)
    sk_dup = jnp.concatenate([sk3, sk3], axis=-1).reshape(d, SWA_KV_HEADS * LANES)
    sv3 = w_sv.reshape(d, SWA_KV_HEADS, SWA_DIM)
    sv_dup = jnp.concatenate([sv3, sv3], axis=-1).reshape(d, SWA_KV_HEADS * LANES)
    w_cat = jnp.concatenate([w_cq, w_ckv, kr_slot, w_lx, w_lg, w_sq, sk_dup, sv_dup, w_g],
                            axis=1).astype(BF16)
    seg_widths = (q_rank, kv_rank, LANES, lru_w, lru_w, SWA_HEADS * SWA_DIM, SWA_KV_HEADS * LANES,
                  SWA_KV_HEADS * LANES, N_BRANCH * d)

    wuq3 = w_uq[l].reshape(q_rank, MLA_HEADS, MLA_QK)
    wuq = jnp.concatenate([wuq3[..., :MLA_NOPE], _rope_slot(wuq3[..., MLA_NOPE:])],
                          axis=-1).reshape(q_rank, MLA_HEADS * LANES).astype(BF16)
    wukv = w_ukv[l].reshape(kv_rank, MLA_HEADS, MLA_NOPE + MLA_V)
    wuk = _pad_last(wukv[..., :MLA_NOPE], LANES).reshape(kv_rank, MLA_HEADS * LANES).astype(BF16)
    wuv = wukv[..., MLA_NOPE:].reshape(kv_rank, MLA_HEADS * MLA_V).astype(BF16)
    gq = jnp.concatenate([mla_q_norm[l][:MLA_NOPE], _rope_slot(mla_q_norm[l][MLA_NOPE:])])[None]
    gk = _pad_last(mla_k_norm[l][None, :MLA_NOPE], LANES)
    gkr = jnp.concatenate([jnp.zeros((MLA_NOPE,), F32), _rope_slot(mla_k_norm[l][MLA_NOPE:])])[None]
    sgq = jnp.concatenate([swa_q_norm[l], swa_q_norm[l]])[None]
    sgk = jnp.concatenate([swa_k_norm[l], swa_k_norm[l]])[None]
    return dict(w_cat=w_cat, seg_widths=seg_widths, wuq=wuq, wuk=wuk, wuv=wuv, gq=gq, gk=gk,
                gkr=gkr, gqa=mla_q_a_norm[l][None, :], gkva=mla_kv_a_norm[l][None, :],
                sgq=sgq, sgk=sgk)


def kernel(x, norm1, w_in, mla_q_a_norm, mla_kv_a_norm, w_uq, w_ukv, mla_q_norm, mla_k_norm,
           conv_w, conv_b, w_rg_a, b_rg_a, w_rg_x, b_rg_x, lru_lambda, swa_q_norm, swa_k_norm,
           swa_sinks, w_branch, w_out, norm2, w_ff1, w_ff2):
    batch, seq, d = x.shape
    assert batch == 1
    depth = w_in.shape[0]
    cos_t, sin_t = _rope_tables(seq)
    h = x.reshape(seq, d)
    for l in range(depth):
        lw = _layer_weights(l, d, w_in, mla_q_a_norm, mla_kv_a_norm, w_uq, w_ukv, mla_q_norm,
                            mla_k_norm, swa_q_norm, swa_k_norm)
        cq, ckv, kr, lx, lg, sq, sk, sv, gates = _in_proj(
            h, norm1[l][None, :], lw["w_cat"], lw["seg_widths"])
        q, kt, v = _mla_qkv(cq, ckv, kr, cos_t, sin_t, lw["gqa"], lw["gkva"], lw["wuq"],
                            lw["wuk"], lw["wuv"], lw["gq"], lw["gk"], lw["gkr"])
        o_a = _mla_attn(q, kt, v)
        o_b = _rglru(lx, lg, conv_w[l], conv_b[l][None, :], w_rg_a[l].astype(BF16),
                     b_rg_a[l][None, :], w_rg_x[l].astype(BF16), b_rg_x[l][None, :],
                     lru_lambda[l][None, :])
        o_c = _swa(swa_sinks[l] * LOG2E, sq, sk, sv, lw["sgq"], lw["sgk"])
        h = _merge(h, o_a, o_b, o_c, gates, w_branch[l].astype(BF16), w_out[l].astype(BF16))
        h = _ffn(h, norm2[l][None, :], w_ff1[l].astype(BF16), w_ff2[l].astype(BF16))
    return h.reshape(batch, seq, d)
```

```python
import functools
import math

import jax
import jax.numpy as jnp
import numpy as np
from jax import lax
from jax.experimental import pallas as pl
from jax.experimental.pallas import tpu as pltpu

F32 = jnp.float32
BF16 = jnp.bfloat16

LANES = 128

MLA_HEADS = 16
MLA_NOPE = 64
MLA_ROPE = 32
MLA_QK = MLA_NOPE + MLA_ROPE
MLA_V = 64
ROPE_THETA = 10000.0
LRU_BLOCKS = 4
CONV_WIDTH = 4
LRU_C = 8.0
SWA_HEADS = 16
SWA_KV_HEADS = 2
SWA_GROUP = SWA_HEADS // SWA_KV_HEADS
SWA_DIM = 64
WINDOW = 128
N_BRANCH = 3
EPS = 1e-6
NEG = -1e30
LOG2E = math.log2(math.e)

VMEM_LIMIT = 56 * 1024 * 1024


def _cparams(sem):
    return pltpu.CompilerParams(dimension_semantics=sem, vmem_limit_bytes=VMEM_LIMIT)


def _const_spec(shape):
    nd = len(shape)
    return pl.BlockSpec(shape, lambda *_: (0,) * nd, pipeline_mode=pl.Buffered(1))


def _rms(xf, gain):
    ms = jnp.mean(xf * xf, axis=-1, keepdims=True)
    return xf * lax.rsqrt(ms + EPS) * gain


def _sigmoid(x):
    return 1.0 / (1.0 + jnp.exp2(x * (-LOG2E)))


def _in_proj_kernel(x_ref, g_ref, w_ref, *out_refs, seg_widths, chunk):
    xn = _rms(x_ref[...], g_ref[...]).astype(BF16)
    lo = 0
    for o_ref, width in zip(out_refs, seg_widths):
        for c in range(0, width, chunk):
            ce = min(c + chunk, width)
            o_ref[:, c:ce] = jnp.dot(
                xn, w_ref[:, lo + c:lo + ce], preferred_element_type=F32).astype(o_ref.dtype)
        lo += width


def _in_proj(x2d, gain, w_cat, seg_widths, tm=512):
    s, d = x2d.shape
    n = w_cat.shape[1]
    assert sum(seg_widths) == n and s % tm == 0
    return pl.pallas_call(
        functools.partial(_in_proj_kernel, seg_widths=tuple(seg_widths), chunk=1024),
        grid=(s // tm,),
        in_specs=[pl.BlockSpec((tm, d), lambda i: (i, 0)),
                  _const_spec((1, d)),
                  _const_spec((d, n))],
        out_specs=[pl.BlockSpec((tm, w), lambda i: (i, 0)) for w in seg_widths],
        out_shape=[jax.ShapeDtypeStruct((s, w), BF16) for w in seg_widths],
        compiler_params=_cparams(("parallel",)),
        name="in_proj",
    )(x2d, gain, w_cat)


def _segment_rsqrt(y, seg_ref, segt_ref, inv_n_ref):
    ss = jnp.dot((y * y).astype(BF16), seg_ref[...], preferred_element_type=F32)
    inv = lax.rsqrt(ss * inv_n_ref[...] + EPS)
    hi = inv.astype(BF16)
    lo = (inv - hi.astype(F32)).astype(BF16)
    return jnp.dot(jnp.concatenate([hi, lo], axis=1), segt_ref[...], preferred_element_type=F32)


def _mla_qkv_kernel(cq_ref, ckv_ref, kr_ref, cos_ref, sin_ref, gqa_ref, gkva_ref,
                    wuq_ref, wuk_ref, wuv_ref, gq_ref, gk_ref, gkr_ref,
                    qseg_ref, qsegt_ref, qn_ref, kseg_ref, ksegt_ref, kn_ref,
                    q_out, kt_out, v_out, *, scale):
    tm = cq_ref.shape[0]
    lane = lax.broadcasted_iota(jnp.int32, (tm, LANES), 1)
    is_rope = (lane >= MLA_NOPE) & (lane < MLA_QK)
    cos = cos_ref[...]
    sin = sin_ref[...]

    def rope(y):
        return y * cos + pltpu.roll(y, LANES - MLA_ROPE // 2, 1) * sin

    cqn = _rms(cq_ref[...].astype(F32), gqa_ref[...]).astype(BF16)
    kvn = _rms(ckv_ref[...].astype(F32), gkva_ref[...]).astype(BF16)

    kr = kr_ref[...].astype(F32)
    kr_ms = jnp.sum(jnp.where(is_rope, kr * kr, 0.0), axis=-1, keepdims=True) * (1.0 / MLA_ROPE)
    kpe = rope(kr * lax.rsqrt(kr_ms + EPS) * gkr_ref[...])

    gq = gq_ref[...] * scale
    gk = gk_ref[...]
    yq = jnp.dot(cqn, wuq_ref[...], preferred_element_type=F32)
    yq = yq * _segment_rsqrt(yq, qseg_ref, qsegt_ref, qn_ref)
    yk = jnp.dot(kvn, wuk_ref[...], preferred_element_type=F32)
    yk = yk * _segment_rsqrt(yk, kseg_ref, ksegt_ref, kn_ref)
    for h in range(MLA_HEADS):
        sl = slice(h * LANES, (h + 1) * LANES)
        q_out[h] = rope(yq[:, sl] * gq).astype(BF16)
        kt_out[h] = (yk[:, sl] * gk + kpe).T.astype(BF16)

    ones = jnp.ones((tm, LANES), BF16)
    for g in range(MLA_HEADS // 4):
        v2 = jnp.dot(kvn, wuv_ref[:, 2 * g * LANES:(2 * g + 2) * LANES],
                     preferred_element_type=F32).astype(BF16)
        for i in range(2):
            v_out[2 * g + i, :, 0:LANES] = v2[:, i * LANES:(i + 1) * LANES]
            v_out[2 * g + i, :, LANES:2 * LANES] = ones


def _segment_indicators(n_slots, segments):
    assert n_slots * len(segments) <= LANES
    seg = np.zeros((n_slots * LANES, LANES), np.float32)
    segt = np.zeros((LANES, n_slots * LANES), np.float32)
    inv_n = np.zeros((1, LANES), np.float32)
    col = 0
    for slot in range(n_slots):
        for start, n_sum, n_spread in segments:
            lo = slot * LANES + start
            seg[lo:lo + n_sum, col] = 1.0
            segt[col, lo:lo + n_spread] = 1.0
            inv_n[0, col] = 1.0 / n_sum
            col += 1
    return (jnp.asarray(seg, BF16), jnp.asarray(np.concatenate([segt, segt], axis=0), BF16),
            jnp.asarray(inv_n))


def _mla_qkv(cq, ckv, kr, cos, sin, gqa, gkva, wuq, wuk, wuv, gq, gk, gkr, tm=256):
    s = cq.shape[0]
    h = MLA_HEADS
    row = lambda w: pl.BlockSpec((tm, w), lambda i: (i, 0))
    qind = _segment_indicators(h, [(0, MLA_NOPE, MLA_NOPE),
                                   (MLA_NOPE, MLA_ROPE, MLA_ROPE + MLA_ROPE // 2)])
    kind = _segment_indicators(h, [(0, MLA_NOPE, MLA_NOPE)])
    return pl.pallas_call(
        functools.partial(_mla_qkv_kernel, scale=MLA_QK ** -0.5 * LOG2E),
        grid=(s // tm,),
        in_specs=[row(cq.shape[1]), row(ckv.shape[1]), row(LANES), row(LANES), row(LANES),
                  _const_spec(gqa.shape), _const_spec(gkva.shape), _const_spec(wuq.shape),
                  _const_spec(wuk.shape), _const_spec(wuv.shape), _const_spec(gq.shape),
                  _const_spec(gk.shape), _const_spec(gkr.shape)]
                 + [_const_spec(a.shape) for a in qind + kind],
        out_specs=[pl.BlockSpec((h, tm, LANES), lambda i: (0, i, 0)),
                   pl.BlockSpec((h, LANES, tm), lambda i: (0, 0, i)),
                   pl.BlockSpec((h // 2, tm, 2 * LANES), lambda i: (0, i, 0))],
        out_shape=[jax.ShapeDtypeStruct((h, s, LANES), BF16),
                   jax.ShapeDtypeStruct((h, LANES, s), BF16),
                   jax.ShapeDtypeStruct((h // 2, s, 2 * LANES), BF16)],
        compiler_params=_cparams(("parallel",)),
        name="mla_qkv",
    )(cq, ckv, kr, cos, sin, gqa, gkva, wuq, wuk, wuv, gq, gk, gkr, *qind, *kind)


def _attn_rows(q_ref, kt_ref, v_ref, m_sc, acc_sc, h, r, rb, diag):
    tk = kt_ref.shape[2]
    ncol = r + rb if diag else tk
    s = jnp.dot(q_ref[h, r:r + rb, :], kt_ref[h, :, 0:ncol], preferred_element_type=F32)
    tiles = [s[:, j * LANES:(j + 1) * LANES] for j in range(ncol // LANES)]
    if diag:
        rowi = lax.broadcasted_iota(jnp.int32, (rb, LANES), 0)
        coli = lax.broadcasted_iota(jnp.int32, (rb, LANES), 1)
        for jj in range(rb // LANES):
            j = r // LANES + jj
            tiles[j] = jnp.where(coli + jj * LANES <= rowi, tiles[j], NEG)
    mloc = functools.reduce(jnp.maximum, tiles)
    m_prev = m_sc[h, r:r + rb, :]
    m_new = jnp.maximum(m_prev, jnp.max(mloc, axis=-1, keepdims=True))
    alpha = jnp.exp2(m_prev - m_new)
    p = jnp.concatenate([jnp.exp2(t - m_new).astype(BF16) for t in tiles], axis=1)
    pv = jnp.dot(p, v_ref[h // 2, 0:ncol, :], preferred_element_type=F32)
    acc_sc[h, r:r + rb, :] = jnp.concatenate([alpha, alpha], axis=1) * acc_sc[h, r:r + rb, :] + pv
    m_sc[h, r:r + rb, :] = m_new


def _mla_attn_kernel(qi_ref, ki_ref, q_ref, kt_ref, v_ref, o_ref, m_sc, acc_sc, *, rb):
    t = pl.program_id(1)
    qi = qi_ref[t]
    ki = ki_ref[t]
    tq = q_ref.shape[1]

    @pl.when(ki == 0)
    def _():
        m_sc[...] = jnp.full(m_sc.shape, NEG, F32)
        acc_sc[...] = jnp.zeros(acc_sc.shape, F32)

    hp = q_ref.shape[0]

    @pl.when(ki < qi)
    def _():
        for h in range(hp):
            for r in range(0, tq, rb):
                _attn_rows(q_ref, kt_ref, v_ref, m_sc, acc_sc, h, r, rb, diag=False)

    @pl.when(ki == qi)
    def _():
        for h in range(hp):
            for r in range(0, tq, rb):
                _attn_rows(q_ref, kt_ref, v_ref, m_sc, acc_sc, h, r, rb, diag=True)
        lane = lax.broadcasted_iota(jnp.int32, (tq, LANES), 1)
        for j in range(hp // 2):
            o0 = acc_sc[2 * j, :, 0:LANES] / acc_sc[2 * j, :, LANES:2 * LANES]
            o1 = acc_sc[2 * j + 1, :, 0:LANES] / acc_sc[2 * j + 1, :, LANES:2 * LANES]
            o_ref[:, j * LANES:(j + 1) * LANES] = jnp.where(lane < MLA_V, o0, o1).astype(o_ref.dtype)


def _mla_attn(q, kt, v, tq=1024, rb=512, hp=8):
    h, s, _ = q.shape
    nq = s // tq
    qi_tab = np.concatenate([np.full(i + 1, i, np.int32) for i in range(nq)])
    ki_tab = np.concatenate([np.arange(i + 1, dtype=np.int32) for i in range(nq)])
    grid_spec = pltpu.PrefetchScalarGridSpec(
        num_scalar_prefetch=2,
        grid=(h // hp, len(qi_tab)),
        in_specs=[pl.BlockSpec((hp, tq, LANES), lambda p, t, qi, ki: (p, qi[t], 0)),
                  pl.BlockSpec((hp, LANES, tq), lambda p, t, qi, ki: (p, 0, ki[t])),
                  pl.BlockSpec((hp // 2, tq, 2 * LANES), lambda p, t, qi, ki: (p, ki[t], 0))],
        out_specs=pl.BlockSpec((tq, hp // 2 * LANES), lambda p, t, qi, ki: (qi[t], p)),
        scratch_shapes=[pltpu.VMEM((hp, tq, LANES), F32),
                        pltpu.VMEM((hp, tq, 2 * LANES), F32)],
    )
    return pl.pallas_call(
        functools.partial(_mla_attn_kernel, rb=rb),
        grid_spec=grid_spec,
        out_shape=jax.ShapeDtypeStruct((s, h * MLA_V), BF16),
        compiler_params=_cparams(("parallel", "arbitrary")),
        name="mla_attn",
    )(jnp.asarray(qi_tab), jnp.asarray(ki_tab), q, kt, v)


def _rglru_kernel(x_ref, gate_ref, cw_ref, cb_ref, wa_ref, ba_ref, wx_ref, bx_ref, lam_ref,
                  o_ref, xext, a_sc, b_sc, h_sc, hcar):
    i = pl.program_id(0)
    tm, width = x_ref.shape
    pad = 8

    @pl.when(i == 0)
    def _():
        xext[0:pad, :] = jnp.zeros((pad, width), F32)
        hcar[...] = jnp.zeros(hcar.shape, F32)

    @pl.when(i > 0)
    def _():
        xext[0:pad, :] = xext[tm:tm + pad, :]

    xext[pad:pad + tm, :] = x_ref[...].astype(F32)
    xc = cb_ref[...]
    for kk in range(CONV_WIDTH):
        xc = xc + cw_ref[kk:kk + 1, :] * xext[pl.ds(pad - (CONV_WIDTH - 1) + kk, tm), :]
    xcb = xc.astype(BF16)

    z = -lam_ref[...]
    softplus = jnp.maximum(z, 0.0) + jnp.log(1.0 + jnp.exp(-jnp.abs(z)))
    log2_a_rate = softplus * (-LRU_C * LOG2E)
    bw = width // LRU_BLOCKS
    for b in range(LRU_BLOCKS):
        sl = slice(b * bw, (b + 1) * bw)
        xb = xcb[:, sl]
        r = _sigmoid(jnp.dot(xb, wa_ref[b], preferred_element_type=F32) + ba_ref[:, sl])
        ig = _sigmoid(jnp.dot(xb, wx_ref[b], preferred_element_type=F32) + bx_ref[:, sl])
        a = jnp.exp2(r * log2_a_rate[:, sl])
        a_sc[:, sl] = a
        b_sc[:, sl] = jnp.sqrt(1.0 - a * a) * (ig * xc[:, sl])

    def step(t, h):
        h = a_sc[pl.ds(t, 1), :] * h + b_sc[pl.ds(t, 1), :]
        h_sc[pl.ds(t, 1), :] = h
        return h

    h_last = lax.fori_loop(0, tm, step, hcar[0:1, :], unroll=8)
    hcar[0:1, :] = h_last

    g = gate_ref[...].astype(F32)
    gelu = 0.5 * g * (1.0 + jnp.tanh(math.sqrt(2.0 / math.pi) * (g + 0.044715 * (g * g * g))))
    o_ref[...] = (h_sc[...] * gelu).astype(o_ref.dtype)


def _rglru(lx, lg, cw, cb, wa, ba, wx, bx, lam, tm=512):
    s, width = lx.shape
    row = pl.BlockSpec((tm, width), lambda i: (i, 0))
    return pl.pallas_call(
        _rglru_kernel,
        grid=(s // tm,),
        in_specs=[row, row, _const_spec(cw.shape), _const_spec(cb.shape), _const_spec(wa.shape),
                  _const_spec(ba.shape), _const_spec(wx.shape), _const_spec(bx.shape),
                  _const_spec(lam.shape)],
        out_specs=row,
        out_shape=jax.ShapeDtypeStruct((s, width), BF16),
        scratch_shapes=[pltpu.VMEM((tm + 8, width), F32), pltpu.VMEM((tm, width), F32),
                        pltpu.VMEM((tm, width), F32), pltpu.VMEM((tm, width), F32),
                        pltpu.VMEM((8, width), F32)],
        compiler_params=_cparams(("arbitrary",)),
        name="rglru",
    )(lx, lg, cw, cb, wa, ba, wx, bx, lam)


def _swa_kernel(sinks_ref, q_ref, kc_ref, kp_ref, vc_ref, vp_ref, gq_ref, gk_ref, o_ref, *, scale):
    n = pl.program_id(0)
    w = q_ref.shape[0]
    qi = lax.broadcasted_iota(jnp.int32, (w, 2 * w), 0)
    kj = lax.broadcasted_iota(jnp.int32, (w, 2 * w), 1)
    delta = qi + w - kj
    valid = (delta >= 0) & (delta < w) & ((n > 0) | (kj >= w))
    lane = lax.broadcasted_iota(jnp.int32, (w, LANES), 1)
    low = lane < SWA_DIM
    klow = lax.broadcasted_iota(jnp.int32, (2 * w, LANES), 1) < SWA_DIM
    gq = gq_ref[...] * scale
    gk = gk_ref[...]
    ones = jnp.ones((2 * w, LANES), BF16)
    for kv in range(SWA_KV_HEADS):
        sl = slice(kv * LANES, (kv + 1) * LANES)
        kk = jnp.concatenate([kp_ref[:, sl], kc_ref[:, sl]], axis=0).astype(F32)
        k_ms = jnp.sum(kk * kk, axis=-1, keepdims=True) * (0.5 / SWA_DIM)
        kk = kk * lax.rsqrt(k_ms + EPS) * gk
        kbd = jnp.concatenate([jnp.where(klow, kk, 0.0), jnp.where(klow, 0.0, kk)],
                              axis=0).astype(BF16)
        vv = jnp.concatenate(
            [jnp.concatenate([vp_ref[:, sl], vc_ref[:, sl]], axis=0), ones], axis=1)
        nslot = SWA_GROUP // 2
        slot0 = kv * nslot
        q = jnp.concatenate([q_ref[:, (slot0 + j) * LANES:(slot0 + j + 1) * LANES]
                             for j in range(nslot)], axis=0).astype(F32)
        low4 = jnp.concatenate([low] * nslot, axis=0)
        q2 = q * q
        ms_lo = jnp.sum(jnp.where(low4, q2, 0.0), axis=-1, keepdims=True) * (1.0 / SWA_DIM)
        ms_hi = jnp.sum(jnp.where(low4, 0.0, q2), axis=-1, keepdims=True) * (1.0 / SWA_DIM)
        r = jnp.where(low4, lax.rsqrt(ms_lo + EPS), lax.rsqrt(ms_hi + EPS))
        qn = (q * r * gq).astype(BF16)
        s2 = lax.dot_general(qn, kbd, (((1,), (1,)), ((), ())), preferred_element_type=F32)
        ps, sink_terms = [], []
        for j in range(nslot):
            for i in range(2):
                s = jnp.where(valid, s2[j * w:(j + 1) * w, i * 2 * w:(i + 1) * 2 * w], NEG)
                sink = sinks_ref[2 * (slot0 + j) + i]
                m = jnp.maximum(jnp.max(s, axis=-1, keepdims=True), sink)
                ps.append(jnp.exp2(s - m).astype(BF16))
                sink_terms.append(jnp.exp2(sink - m))
        pv = jnp.dot(jnp.concatenate(ps, axis=0), vv, preferred_element_type=F32)
        for j in range(nslot):
            outs = []
            for i in range(2):
                rows = slice((2 * j + i) * w, (2 * j + i + 1) * w)
                outs.append(pv[rows, 0:LANES] / (pv[rows, LANES:2 * LANES] + sink_terms[2 * j + i]))
            o_ref[:, (slot0 + j) * LANES:(slot0 + j + 1) * LANES] = jnp.where(
                low, outs[0], outs[1]).astype(o_ref.dtype)


def _swa(sinks, sq, sk, sv, gq, gk):
    s = sq.shape[0]
    w = WINDOW
    cur = lambda width: pl.BlockSpec((w, width), lambda n: (n, 0))
    prev = lambda width: pl.BlockSpec((w, width), lambda n: (jnp.maximum(n - 1, 0), 0))
    return pl.pallas_call(
        functools.partial(_swa_kernel, scale=SWA_DIM ** -0.5 * LOG2E),
        grid=(s // w,),
        in_specs=[pl.BlockSpec(memory_space=pltpu.SMEM),
                  cur(sq.shape[1]), cur(sk.shape[1]), prev(sk.shape[1]),
                  cur(sv.shape[1]), prev(sv.shape[1]),
                  _const_spec(gq.shape), _const_spec(gk.shape)],
        out_specs=cur(SWA_HEADS * SWA_DIM),
        out_shape=jax.ShapeDtypeStruct((s, SWA_HEADS * SWA_DIM), BF16),
        compiler_params=_cparams(("parallel",)),
        name="swa",
    )(sinks, sq, sk, sk, sv, sv, gq, gk)


def _merge_ffn_kernel(x_ref, oa_ref, ob_ref, oc_ref, g_ref, wb_ref, wo_ref, g2_ref, w1_ref, w2_ref,
                      y_ref, *, chunk):
    d = x_ref.shape[1]
    merged = None
    for b, o_ref in enumerate((oa_ref, ob_ref, oc_ref)):
        br = jnp.dot(o_ref[...], wb_ref[b], preferred_element_type=F32)
        term = _sigmoid(g_ref[:, b * d:(b + 1) * d].astype(F32)) * br
        merged = term if merged is None else merged + term
    x = x_ref[...] + jnp.dot(merged.astype(BF16), wo_ref[...], preferred_element_type=F32)
    hn = _rms(x, g2_ref[...]).astype(BF16)
    acc = x
    for c in range(0, w1_ref.shape[1], chunk):
        a = jnp.maximum(jnp.dot(hn, w1_ref[:, c:c + chunk], preferred_element_type=F32), 0.0)
        acc = acc + jnp.dot((a * a).astype(BF16), w2_ref[c:c + chunk, :],
                            preferred_element_type=F32)
    y_ref[...] = acc


def _merge_ffn(x2d, oa, ob, oc, gates, wb, wo, gain2, w1, w2, tm=512):
    s, d = x2d.shape
    row = lambda w: pl.BlockSpec((tm, w), lambda i: (i, 0))
    return pl.pallas_call(
        functools.partial(_merge_ffn_kernel, chunk=1024),
        grid=(s // tm,),
        in_specs=[row(d), row(d), row(d), row(d), row(N_BRANCH * d),
                  _const_spec(wb.shape), _const_spec(wo.shape), _const_spec(gain2.shape),
                  _const_spec(w1.shape), _const_spec(w2.shape)],
        out_specs=row(d),
        out_shape=jax.ShapeDtypeStruct((s, d), F32),
        compiler_params=_cparams(("parallel",)),
        name="merge_ffn",
    )(x2d, oa, ob, oc, gates, wb, wo, gain2, w1, w2)


def _pad_last(a, width):
    return jnp.pad(a, [(0, 0)] * (a.ndim - 1) + [(0, width - a.shape[-1])])


def _rope_tables(seq):
    half = MLA_ROPE // 2
    pos = jnp.arange(seq, dtype=F32)
    inv = ROPE_THETA ** (-jnp.arange(0, MLA_ROPE, 2, dtype=F32) / MLA_ROPE)
    ang = pos[:, None] * inv[None, :]
    cos, sin = lax.optimization_barrier((jnp.cos(ang), jnp.sin(ang)))
    assert cos.shape[1] == half
    ones = jnp.ones((seq, MLA_NOPE), F32)
    tail = jnp.zeros((seq, LANES - MLA_QK), F32)
    nope0 = jnp.zeros((seq, MLA_NOPE), F32)
    cos_t = jnp.concatenate([ones, cos, cos, tail], axis=1)
    sin_t = jnp.concatenate([nope0, -sin, sin, tail], axis=1)
    return cos_t, sin_t


def _rope_slot(a):
    half = MLA_ROPE // 2
    return _pad_last(jnp.concatenate([a, a[..., :half]], axis=-1), LANES - MLA_NOPE)


def _layer_weights(l, d, w_in, mla_q_a_norm, mla_kv_a_norm, w_uq, w_ukv, mla_q_norm, mla_k_norm,
                   swa_q_norm, swa_k_norm):
    q_rank = w_uq.shape[1]
    kv_rank = w_ukv.shape[1]
    lru_w = d
    sizes = (q_rank, kv_rank, MLA_ROPE, lru_w, lru_w, SWA_HEADS * SWA_DIM,
             SWA_KV_HEADS * SWA_DIM, SWA_KV_HEADS * SWA_DIM, N_BRANCH * d)
    assert sum(sizes) == w_in.shape[2]
    offs = np.cumsum((0,) + sizes)
    seg = [w_in[l][:, offs[i]:offs[i + 1]] for i in range(len(sizes))]
    w_cq, w_ckv, w_kr, w_lx, w_lg, w_sq, w_sk, w_sv, w_g = seg
    kr_slot = jnp.concatenate([jnp.zeros((d, MLA_NOPE), F32), _rope_slot(w_kr)], axis=1)
    sk3 = w_sk.reshape(d, SWA_KV_HEADS, SWA_DIM)
    sk_dup = jnp.concatenate([sk3, sk3], axis=-1).reshape(d, SWA_KV_HEADS * LANES)
    sv3 = w_sv.reshape(d, SWA_KV_HEADS, SWA_DIM)
    sv_dup = jnp.concatenate([sv3, sv3], axis=-1).reshape(d, SWA_KV_HEADS * LANES)
    w_cat = jnp.concatenate([w_cq, w_ckv, kr_slot, w_lx, w_lg, w_sq, sk_dup, sv_dup, w_g],
                            axis=1).astype(BF16)
    seg_widths = (q_rank, kv_rank, LANES, lru_w, lru_w, SWA_HEADS * SWA_DIM, SWA_KV_HEADS * LANES,
                  SWA_KV_HEADS * LANES, N_BRANCH * d)

    wuq3 = w_uq[l].reshape(q_rank, MLA_HEADS, MLA_QK)
    wuq = jnp.concatenate([wuq3[..., :MLA_NOPE], _rope_slot(wuq3[..., MLA_NOPE:])],
                          axis=-1).reshape(q_rank, MLA_HEADS * LANES).astype(BF16)
    wukv = w_ukv[l].reshape(kv_rank, MLA_HEADS, MLA_NOPE + MLA_V)
    wuk = _pad_last(wukv[..., :MLA_NOPE], LANES).reshape(kv_rank, MLA_HEADS * LANES).astype(BF16)
    wuv = wukv[..., MLA_NOPE:].reshape(kv_rank, MLA_HEADS * MLA_V).astype(BF16)
    gq = jnp.concatenate([mla_q_norm[l][:MLA_NOPE], _rope_slot(mla_q_norm[l][MLA_NOPE:])])[None]
    gk = _pad_last(mla_k_norm[l][None, :MLA_NOPE], LANES)
    gkr = jnp.concatenate([jnp.zeros((MLA_NOPE,), F32), _rope_slot(mla_k_norm[l][MLA_NOPE:])])[None]
    sgq = jnp.concatenate([swa_q_norm[l], swa_q_norm[l]])[None]
    sgk = jnp.concatenate([swa_k_norm[l], swa_k_norm[l]])[None]
    return dict(w_cat=w_cat, seg_widths=seg_widths, wuq=wuq, wuk=wuk, wuv=wuv, gq=gq, gk=gk,
                gkr=gkr, gqa=mla_q_a_norm[l][None, :], gkva=mla_kv_a_norm[l][None, :],
                sgq=sgq, sgk=sgk)


def kernel(x, norm1, w_in, mla_q_a_norm, mla_kv_a_norm, w_uq, w_ukv, mla_q_norm, mla_k_norm,
           conv_w, conv_b, w_rg_a, b_rg_a, w_rg_x, b_rg_x, lru_lambda, swa_q_norm, swa_k_norm,
           swa_sinks, w_branch, w_out, norm2, w_ff1, w_ff2):
    batch, seq, d = x.shape
    assert batch == 1
    depth = w_in.shape[0]
    cos_t, sin_t = _rope_tables(seq)
    h = x.reshape(seq, d)
    for l in range(depth):
        lw = _layer_weights(l, d, w_in, mla_q_a_norm, mla_kv_a_norm, w_uq, w_ukv, mla_q_norm,
                            mla_k_norm, swa_q_norm, swa_k_norm)
        cq, ckv, kr, lx, lg, sq, sk, sv, gates = _in_proj(
            h, norm1[l][None, :], lw["w_cat"], lw["seg_widths"])
        q, kt, v = _mla_qkv(cq, ckv, kr, cos_t, sin_t, lw["gqa"], lw["gkva"], lw["wuq"],
                            lw["wuk"], lw["wuv"], lw["gq"], lw["gk"], lw["gkr"])
        o_a = _mla_attn(q, kt, v)
        o_b = _rglru(lx, lg, conv_w[l], conv_b[l][None, :], w_rg_a[l].astype(BF16),
                     b_rg_a[l][None, :], w_rg_x[l].astype(BF16), b_rg_x[l][None, :],
                     lru_lambda[l][None, :])
        o_c = _swa(swa_sinks[l] * LOG2E, sq, sk, sv, lw["sgq"], lw["sgk"])
        h = _merge_ffn(h, o_a, o_b, o_c, gates, w_branch[l].astype(BF16), w_out[l].astype(BF16),
                       norm2[l][None, :], w_ff1[l].astype(BF16), w_ff2[l].astype(BF16))
    return h.reshape(batch, seq, d)
```
